```python
import jax, jax.numpy as jnp
from jax import lax
import numpy as np

D_MODEL = 1024
BATCH = 8
SEQ = 4096
DEPTH = 2

ATTN_HEADS = 8
ATTN_KV_HEADS = 2
ATTN_HEAD_DIM = 64
WINDOW = 128
GDN_HEADS = 4
GDN_HEAD_DIM = 128
GDN_CHUNK = 64
GDN_CONV = 4
D_FF = 2816
FFN_CONV = 3
NORM_EPS = 1e-6

ATTN_Q_DIM = ATTN_HEADS * ATTN_HEAD_DIM
ATTN_KV_DIM = ATTN_KV_HEADS * ATTN_HEAD_DIM
GDN_DIM = GDN_HEADS * GDN_HEAD_DIM
MIX_DIM = ATTN_Q_DIM + GDN_DIM
IN_SPLITS = (ATTN_Q_DIM, ATTN_KV_DIM, ATTN_KV_DIM, 3 * GDN_DIM, GDN_DIM, GDN_HEADS, GDN_HEADS)
IN_DIM = sum(IN_SPLITS)

kernel_name = "hybrid_swa_sink_alibi_gdn_convffn"


def rmsnorm(x, w):
    xf = x.astype(jnp.float32)
    y = xf * lax.rsqrt(jnp.mean(xf * xf, axis=-1, keepdims=True) + NORM_EPS)
    return (y * w.astype(jnp.float32)).astype(x.dtype)


def l2norm(x):
    return x * lax.rsqrt(jnp.sum(x * x, axis=-1, keepdims=True) + NORM_EPS)


def causal_dwconv(x, w):
    K, C = w.shape
    return lax.conv_general_dilated(
        x, w.astype(x.dtype)[:, None, :], window_strides=(1,), padding=[(K - 1, 0)],
        dimension_numbers=("NWC", "WIO", "NWC"), feature_group_count=C)


def alibi_slopes(n):
    return 2.0 ** (-8.0 * jnp.arange(1, n + 1, dtype=jnp.float32) / n)


def sliding_window_attention(q, k, v, sinks):
    B, T, Hq, Dh = q.shape
    Hkv = k.shape[2]
    G = Hq // Hkv
    W = WINDOW
    NB = T // W
    qb = q.reshape(B, NB, W, Hkv, G, Dh)

    def with_prev(x):
        xb = x.reshape(B, NB, W, Hkv, Dh)
        prev = jnp.pad(xb[:, :-1], ((0, 0), (1, 0), (0, 0), (0, 0), (0, 0)))
        return jnp.concatenate([prev, xb], axis=2)

    kb, vb = with_prev(k), with_prev(v)
    s = jnp.einsum("bnqhgd,bnkhd->bnhgqk", qb, kb).astype(jnp.float32) * (Dh ** -0.5)
    qpos = jnp.arange(W)[:, None] + W
    kpos = jnp.arange(2 * W)[None, :]
    rel = qpos - kpos
    band = (rel >= 0) & (rel < W)
    blk = jnp.arange(NB)[:, None, None]
    valid = band[None] & ((blk > 0) | (kpos >= W)[None])
    slopes = alibi_slopes(Hq).reshape(Hkv, G)
    alibi = -slopes[:, :, None, None] * rel.astype(jnp.float32)[None, None]
    s = jnp.where(valid[None, :, None, None], s + alibi, -jnp.inf)
    sink = jnp.broadcast_to(sinks.astype(jnp.float32).reshape(Hkv, G)[:, :, None, None], s.shape[:-1] + (1,))
    p = jax.nn.softmax(jnp.concatenate([s, sink], axis=-1), axis=-1)[..., :-1]
    o = jnp.einsum("bnhgqk,bnkhd->bnqhgd", p.astype(v.dtype), vb)
    return o.reshape(B, T, Hq * Dh)


def gated_delta_rule_chunked(q, k, v, g, beta):
    B, T, H, Dk = q.shape
    Dv = v.shape[-1]
    C = GDN_CHUNK
    N = T // C
    q = q * (Dk ** -0.5)

    def chunkify(x):
        return x.reshape(B, N, C, H, -1).transpose(0, 3, 1, 2, 4)

    qc, kc, vc = chunkify(q), chunkify(k), chunkify(v)
    gc = g.reshape(B, N, C, H).transpose(0, 3, 1, 2)
    bc = beta.reshape(B, N, C, H).transpose(0, 3, 1, 2)
    Gc = jnp.cumsum(gc, axis=-1)
    causal = jnp.tril(jnp.ones((C, C), dtype=bool))
    strict = jnp.tril(jnp.ones((C, C), dtype=bool), -1)
    decay = jnp.exp(jnp.where(causal, Gc[..., :, None] - Gc[..., None, :], -jnp.inf))
    kbeta = kc * bc[..., None]
    A = jnp.where(strict, jnp.einsum("bhncd,bhnsd->bhncs", kbeta, kc) * decay, 0.0)
    rhs = jnp.concatenate([vc * bc[..., None], kbeta * jnp.exp(Gc)[..., None]], axis=-1)
    sol = lax.linalg.triangular_solve(A, rhs, left_side=True, lower=True, unit_diagonal=True)
    u = sol[..., :Dv]
    w = sol[..., Dv:]
    qk = jnp.einsum("bhncd,bhnsd->bhncs", qc, kc) * decay
    qd = qc * jnp.exp(Gc)[..., None]
    kd = kc * jnp.exp(Gc[..., -1:] - Gc)[..., None]
    glast = jnp.exp(Gc[..., -1])

    def step(S, xs):
        qk_i, qd_i, w_i, u_i, kd_i, gl_i = xs
        v_new = u_i - jnp.einsum("bhck,bhkv->bhcv", w_i, S)
        o_i = jnp.einsum("bhck,bhkv->bhcv", qd_i, S) + jnp.einsum("bhcs,bhsv->bhcv", qk_i, v_new)
        S = S * gl_i[..., None, None] + jnp.einsum("bhck,bhcv->bhkv", kd_i, v_new)
        return S, o_i

    xs = tuple(jnp.moveaxis(a, 2, 0) for a in (qk, qd, w, u, kd, glast))
    S0 = jnp.zeros((B, H, Dk, Dv), jnp.float32)
    _, o = lax.scan(step, S0, xs)
    return o.transpose(1, 0, 3, 2, 4).reshape(B, T, H, Dv)


def gated_deltanet(qkv, z, b, a, conv_w, a_log, dt_bias, norm_w):
    B, T, _ = qkv.shape
    qkv = jax.nn.silu(causal_dwconv(qkv, conv_w)).astype(jnp.float32)
    qg, kg, vg = jnp.split(qkv, 3, axis=-1)
    qg = l2norm(qg.reshape(B, T, GDN_HEADS, GDN_HEAD_DIM))
    kg = l2norm(kg.reshape(B, T, GDN_HEADS, GDN_HEAD_DIM))
    vg = vg.reshape(B, T, GDN_HEADS, GDN_HEAD_DIM)
    beta = jax.nn.sigmoid(b.astype(jnp.float32))
    g = -jnp.exp(a_log.astype(jnp.float32)) * jax.nn.softplus(a.astype(jnp.float32) + dt_bias.astype(jnp.float32))
    o = gated_delta_rule_chunked(qg, kg, vg, g, beta)
    o = o * lax.rsqrt(jnp.mean(o * o, axis=-1, keepdims=True) + NORM_EPS) * norm_w.astype(jnp.float32)
    o = o * jax.nn.silu(z.astype(jnp.float32).reshape(B, T, GDN_HEADS, GDN_HEAD_DIM))
    return o.reshape(B, T, GDN_DIM).astype(z.dtype)


def conv_gated_mlp(h, w_in, conv_w, conv_b, w_down):
    gate, up = jnp.split(h @ w_in, 2, axis=-1)
    gate = causal_dwconv(gate, conv_w) + conv_b
    return (jax.nn.silu(gate) * up) @ w_down


def setup_inputs(seed: int = 0) -> dict:
    key = jax.random.key(seed)
    ks = jax.random.split(key, 16)
    f32 = jnp.float32
    nrm = lambda k, s, sc: jax.random.normal(k, s, f32) * sc
    dt = jnp.exp(jax.random.uniform(ks[6], (DEPTH, GDN_HEADS), f32, np.log(1e-3), np.log(1e-1)))
    return {
        "x": jax.random.normal(ks[0], (BATCH, SEQ, D_MODEL), f32),
        "attn_norm": 1.0 + nrm(ks[1], (DEPTH, D_MODEL), 0.02),
        "w_in": nrm(ks[2], (DEPTH, D_MODEL, IN_DIM), D_MODEL ** -0.5),
        "attn_sinks": nrm(ks[3], (DEPTH, ATTN_HEADS), 0.5),
        "gdn_conv_w": nrm(ks[4], (DEPTH, GDN_CONV, 3 * GDN_DIM), GDN_CONV ** -0.5),
        "gdn_a_log": jnp.log(jax.random.uniform(ks[5], (DEPTH, GDN_HEADS), f32, 1.0, 16.0)),
        "gdn_dt_bias": dt + jnp.log(-jnp.expm1(-dt)),
        "gdn_norm": 1.0 + nrm(ks[7], (DEPTH, GDN_HEAD_DIM), 0.02),
        "w_out": nrm(ks[8], (DEPTH, MIX_DIM, D_MODEL), MIX_DIM ** -0.5),
        "ffn_norm": 1.0 + nrm(ks[9], (DEPTH, D_MODEL), 0.02),
        "w_ffn_in": nrm(ks[10], (DEPTH, D_MODEL, 2 * D_FF), D_MODEL ** -0.5),
        "ffn_conv_w": nrm(ks[11], (DEPTH, FFN_CONV, D_FF), FFN_CONV ** -0.5),
        "ffn_conv_b": nrm(ks[12], (DEPTH, D_FF), 0.01),
        "w_down": nrm(ks[13], (DEPTH, D_FF, D_MODEL), D_FF ** -0.5),
        "final_norm": 1.0 + nrm(ks[14], (D_MODEL,), 0.02),
    }


def reference(x, attn_norm, w_in, attn_sinks, gdn_conv_w, gdn_a_log, gdn_dt_bias, gdn_norm,
              w_out, ffn_norm, w_ffn_in, ffn_conv_w, ffn_conv_b, w_down, final_norm):
    B, T, _ = x.shape
    cuts = np.cumsum(IN_SPLITS)[:-1].tolist()
    for l in range(DEPTH):
        h = rmsnorm(x, attn_norm[l])
        qa, ka, va, qkv_g, z, b, a = jnp.split(h @ w_in[l], cuts, axis=-1)
        attn_out = sliding_window_attention(
            qa.reshape(B, T, ATTN_HEADS, ATTN_HEAD_DIM),
            ka.reshape(B, T, ATTN_KV_HEADS, ATTN_HEAD_DIM),
            va.reshape(B, T, ATTN_KV_HEADS, ATTN_HEAD_DIM),
            attn_sinks[l])
        gdn_out = gated_deltanet(qkv_g, z, b, a, gdn_conv_w[l], gdn_a_log[l], gdn_dt_bias[l], gdn_norm[l])
        mixed = jnp.concatenate([attn_out.astype(x.dtype), gdn_out.astype(x.dtype)], axis=-1)
        x = x + mixed @ w_out[l]
        x = x + conv_gated_mlp(rmsnorm(x, ffn_norm[l]), w_ffn_in[l], ffn_conv_w[l], ffn_conv_b[l], w_down[l])
    return rmsnorm(x, final_norm)
```

```python
import functools

import numpy as np
import jax
import jax.numpy as jnp
from jax import lax
from jax.experimental import pallas as pl
from jax.experimental.pallas import tpu as pltpu

D_MODEL = 1024
DEPTH = 2
ATTN_HEADS = 8
ATTN_KV_HEADS = 2
ATTN_GROUP = ATTN_HEADS // ATTN_KV_HEADS
ATTN_HEAD_DIM = 64
WINDOW = 128
GDN_HEADS = 4
GDN_HEAD_DIM = 128
GDN_CHUNK = 64
GDN_CONV = 4
D_FF = 2816
FFN_CONV = 3
NORM_EPS = 1e-6

ATTN_Q_DIM = ATTN_HEADS * ATTN_HEAD_DIM
ATTN_KV_DIM = ATTN_KV_HEADS * ATTN_HEAD_DIM
ATTN_QKV_DIM = ATTN_Q_DIM + 2 * ATTN_KV_DIM
GDN_DIM = GDN_HEADS * GDN_HEAD_DIM
GDN_QKV_DIM = 3 * GDN_DIM
MIX_DIM = ATTN_Q_DIM + GDN_DIM
IN_DIM = ATTN_QKV_DIM + GDN_QKV_DIM + GDN_DIM + 2 * GDN_HEADS
LANES = 128
SUBLANES = 8
GATE_DIM = LANES
IN_PAD = IN_DIM - 2 * GDN_HEADS + GATE_DIM

IN_TM = 512
GDN_TT = 256
FFN_TM = 512
FFN_FC = 256
VMEM_LIMIT = 56 * 1024 * 1024

NEG_BIG = -1e30
BF16 = jnp.bfloat16
F32 = jnp.float32


def _mm(a, b):
    return jnp.dot(a.astype(BF16), b.astype(BF16), preferred_element_type=F32)


def _mm_nt(a, b):
    return lax.dot_general(a.astype(BF16), b.astype(BF16), (((1,), (1,)), ((), ())),
                           preferred_element_type=F32)


def _mm_tn(a, b):
    return lax.dot_general(a.astype(BF16), b.astype(BF16), (((0,), (0,)), ((), ())),
                           preferred_element_type=F32)


def _split3(a):
    a1 = a.astype(BF16)
    r1 = a - a1.astype(F32)
    a2 = r1.astype(BF16)
    a3 = (r1 - a2.astype(F32)).astype(BF16)
    return a1, a2, a3


def _mm_hp(a, b):
    a1, a2, a3 = _split3(a)
    b1, b2, b3 = _split3(b)
    dot = functools.partial(jnp.dot, preferred_element_type=F32)
    small = dot(a2, b2) + dot(a1, b3) + dot(a3, b1)
    return (dot(a1, b2) + dot(a2, b1) + small) + dot(a1, b1)


def _rmsnorm(x, w):
    return x * lax.rsqrt(jnp.mean(x * x, axis=-1, keepdims=True) + NORM_EPS) * w


def _silu(x):
    return x / (1.0 + jnp.exp(-x))


def _inproj_kernel(x_ref, nw_ref, w_ref, qkv_ref, g_ref, z_ref, ba_ref):
    h = _rmsnorm(x_ref[...], nw_ref[...]).astype(BF16)
    o0, o1, o2, o3 = 0, ATTN_QKV_DIM, ATTN_QKV_DIM + GDN_QKV_DIM, IN_PAD - GATE_DIM
    qkv_ref[...] = jnp.dot(h, w_ref[:, o0:o1], preferred_element_type=F32).astype(BF16)
    g_ref[...] = jnp.dot(h, w_ref[:, o1:o2], preferred_element_type=F32).astype(BF16)
    z_ref[...] = jnp.dot(h, w_ref[:, o2:o3], preferred_element_type=F32).astype(BF16)
    ba_ref[...] = jnp.dot(h, w_ref[:, o3:IN_PAD], preferred_element_type=F32)


def _inproj(x, norm_w, w_pad):
    m = x.shape[0]
    tm = min(IN_TM, m)
    row = lambda i: (i, 0)
    fixed = lambda i: (0, 0)
    return pl.pallas_call(
        _inproj_kernel,
        grid=(m // tm,),
        in_specs=[pl.BlockSpec((tm, D_MODEL), row),
                  pl.BlockSpec((1, D_MODEL), fixed),
                  pl.BlockSpec((D_MODEL, IN_PAD), fixed)],
        out_specs=[pl.BlockSpec((tm, ATTN_QKV_DIM), row),
                   pl.BlockSpec((tm, GDN_QKV_DIM), row),
                   pl.BlockSpec((tm, GDN_DIM), row),
                   pl.BlockSpec((tm, GATE_DIM), row)],
        out_shape=[jax.ShapeDtypeStruct((m, ATTN_QKV_DIM), BF16),
                   jax.ShapeDtypeStruct((m, GDN_QKV_DIM), BF16),
                   jax.ShapeDtypeStruct((m, GDN_DIM), BF16),
                   jax.ShapeDtypeStruct((m, GATE_DIM), F32)],
        compiler_params=pltpu.CompilerParams(dimension_semantics=("parallel",),
                                             vmem_limit_bytes=VMEM_LIMIT),
        name="inproj",
    )(x, norm_w.reshape(1, D_MODEL), w_pad)


def _attn_bias():
    qpos = np.arange(WINDOW)[:, None] + WINDOW
    kpos = np.arange(2 * WINDOW)[None, :]
    rel = qpos - kpos
    band = (rel >= 0) & (rel < WINDOW)
    slopes = 2.0 ** (-8.0 * np.arange(1, ATTN_HEADS + 1) / ATTN_HEADS)
    alibi = -slopes[:, None, None] * rel[None].astype(np.float64)
    rest = np.where(band[None], alibi, NEG_BIG)
    first = np.where((band & (kpos >= WINDOW))[None], alibi, NEG_BIG)
    return jnp.asarray(np.stack([first, rest]), dtype=F32)


def _attn_kernel(sink_ref, q_ref, kvc_ref, kvp_ref, bias_ref, o_ref):
    dh = ATTN_HEAD_DIM
    for h in range(ATTN_KV_HEADS):
        k = jnp.concatenate([kvp_ref[:, h * dh:(h + 1) * dh], kvc_ref[:, h * dh:(h + 1) * dh]], axis=0)
        vo = ATTN_KV_DIM + h * dh
        v = jnp.concatenate([kvp_ref[:, vo:vo + dh], kvc_ref[:, vo:vo + dh]], axis=0)
        for g in range(ATTN_GROUP):
            hq = h * ATTN_GROUP + g
            q = q_ref[:, hq * dh:(hq + 1) * dh]
            s = _mm_nt(q, k) * (dh ** -0.5) + bias_ref[hq]
            sink = sink_ref[hq]
            m = jnp.maximum(jnp.max(s, axis=-1, keepdims=True), sink)
            p = jnp.exp(s - m)
            denom = jnp.sum(p, axis=-1, keepdims=True) + jnp.exp(sink - m)
            o = _mm(p, v) / denom
            o_ref[:, hq * dh:(hq + 1) * dh] = o.astype(BF16)


def _attention(qkv, sinks, bias, batch, seq):
    m = qkv.shape[0]
    nb = seq // WINDOW
    kv_col = ATTN_Q_DIM // (2 * ATTN_KV_DIM)
    return pl.pallas_call(
        _attn_kernel,
        grid=(batch, nb),
        in_specs=[pl.BlockSpec(memory_space=pltpu.SMEM),
                  pl.BlockSpec((WINDOW, ATTN_Q_DIM), lambda b, i: (b * nb + i, 0)),
                  pl.BlockSpec((WINDOW, 2 * ATTN_KV_DIM), lambda b, i: (b * nb + i, kv_col)),
                  pl.BlockSpec((WINDOW, 2 * ATTN_KV_DIM), lambda b, i: (b * nb + jnp.maximum(i - 1, 0), kv_col)),
                  pl.BlockSpec((None, ATTN_HEADS, WINDOW, 2 * WINDOW), lambda b, i: (jnp.minimum(i, 1), 0, 0, 0))],
        out_specs=pl.BlockSpec((WINDOW, ATTN_Q_DIM), lambda b, i: (b * nb + i, 0)),
        out_shape=jax.ShapeDtypeStruct((m, ATTN_Q_DIM), BF16),
        compiler_params=pltpu.CompilerParams(dimension_semantics=("parallel", "arbitrary"),
                                             vmem_limit_bytes=VMEM_LIMIT),
        name="swa_attention",
    )(sinks, qkv, qkv, qkv, bias)


def _gdn_kernel(g_ref, z_ref, ba_ref, cw_ref, gp_ref, nw_ref, o_ref, xbuf_ref, state_ref):
    tt = g_ref.shape[0]
    nchunk = tt // GDN_CHUNK
    hd = GDN_HEAD_DIM
    halo = SUBLANES

    @pl.when(pl.program_id(1) == 0)
    def _():
        state_ref[...] = jnp.zeros_like(state_ref)
        xbuf_ref[0:halo, :] = jnp.zeros((halo, GDN_QKV_DIM), F32)

    xbuf_ref[halo:halo + tt, :] = g_ref[...].astype(F32)

    def conv_silu(col):
        acc = None
        for j in range(GDN_CONV):
            sh = GDN_CONV - 1 - j
            term = xbuf_ref[halo - sh:halo - sh + tt, col:col + hd] * cw_ref[j:j + 1, col:col + hd]
            acc = term if acc is None else acc + term
        return _silu(acc)

    def l2norm(x):
        return x * lax.rsqrt(jnp.sum(x * x, axis=-1, keepdims=True) + NORM_EPS)

    ba = ba_ref[...]
    beta_all = 1.0 / (1.0 + jnp.exp(-ba))
    pre = ba + gp_ref[1:2, :]
    softplus = jnp.maximum(pre, 0.0) + jnp.log(1.0 + jnp.exp(-jnp.abs(pre)))
    g_all = -jnp.exp(gp_ref[0:1, :]) * softplus

    ri = lax.broadcasted_iota(jnp.int32, (tt, tt), 0)
    ci = lax.broadcasted_iota(jnp.int32, (tt, tt), 1)
    same_chunk = (ri // GDN_CHUNK) == (ci // GDN_CHUNK)
    causal = same_chunk & (ri >= ci)
    strict = same_chunk & (ri > ci)

    ltri = jnp.where(causal, 1.0, 0.0).astype(BF16)
    g1, g2, g3 = _split3(g_all)
    dot = functools.partial(jnp.dot, preferred_element_type=F32)
    gc_all = (dot(ltri, g3) + dot(ltri, g2)) + dot(ltri, g1)
    gc_rows = gc_all.T

    eye = jnp.where(ri == ci, 1.0, 0.0)

    for h in range(GDN_HEADS):
        q = l2norm(conv_silu(h * hd)) * (hd ** -0.5)
        k = l2norm(conv_silu(GDN_DIM + h * hd))
        v = conv_silu(2 * GDN_DIM + h * hd)
        beta = beta_all[:, h:h + 1]
        gc = gc_all[:, GDN_HEADS + h:GDN_HEADS + h + 1]
        gc_row = gc_rows[GDN_HEADS + h:GDN_HEADS + h + 1, :]
        egc = jnp.exp(gc)

        decay = jnp.exp(jnp.where(causal, gc - gc_row, NEG_BIG))
        kb = k * beta
        a_mat = jnp.where(strict, _mm_nt(kb, k) * decay, 0.0)
        qk = _mm_nt(q, k) * decay

        tinv = eye - jnp.where((ri // 2) == (ci // 2), a_mat, 0.0)
        s = 2
        while s < GDN_CHUNK:
            lower_left = ((ri // (2 * s)) == (ci // (2 * s))) & ((ri // s) > (ci // s))
            tinv = tinv - _mm_hp(tinv, _mm_hp(jnp.where(lower_left, a_mat, 0.0), tinv))
            s *= 2

        u = _mm_hp(tinv, v * beta)
        w = _mm_hp(tinv, kb * egc)
        qd = q * egc

        outs = []
        state = state_ref[h]
        for c in range(nchunk):
            r0, r1 = c * GDN_CHUNK, (c + 1) * GDN_CHUNK
            gc_last = gc[r1 - 1:r1, :]
            v_new = u[r0:r1] - _mm(w[r0:r1], state)
            outs.append(_mm(qd[r0:r1], state) + _mm(qk[r0:r1, r0:r1], v_new))
            kd = k[r0:r1] * jnp.exp(gc_last - gc[r0:r1])
            state = state * jnp.exp(gc_last) + _mm_tn(kd, v_new)
        state_ref[h] = state

        o = jnp.concatenate(outs, axis=0)
        o = o * lax.rsqrt(jnp.mean(o * o, axis=-1, keepdims=True) + NORM_EPS) * nw_ref[...]
        zz = z_ref[:, h * hd:(h + 1) * hd].astype(F32)
        o_ref[:, h * hd:(h + 1) * hd] = (o * _silu(zz)).astype(BF16)

    xbuf_ref[0:halo, :] = xbuf_ref[tt:tt + halo, :]


def _gdn(g, z, ba, conv_w, a_log, dt_bias, norm_w, batch, seq):
    m = g.shape[0]
    tt = min(GDN_TT, seq)
    nt = seq // tt
    gate_params = jnp.zeros((SUBLANES, GATE_DIM), F32)
    gate_params = gate_params.at[0, GDN_HEADS:2 * GDN_HEADS].set(a_log.astype(F32))
    gate_params = gate_params.at[1, GDN_HEADS:2 * GDN_HEADS].set(dt_bias.astype(F32))
    row = lambda b, t: (b * nt + t, 0)
    fixed = lambda b, t: (0, 0)
    return pl.pallas_call(
        _gdn_kernel,
        grid=(batch, nt),
        in_specs=[pl.BlockSpec((tt, GDN_QKV_DIM), row),
                  pl.BlockSpec((tt, GDN_DIM), row),
                  pl.BlockSpec((tt, GATE_DIM), row),
                  pl.BlockSpec((GDN_CONV, GDN_QKV_DIM), fixed),
                  pl.BlockSpec((SUBLANES, GATE_DIM), fixed),
                  pl.BlockSpec((1, GDN_HEAD_DIM), fixed)],
        out_specs=pl.BlockSpec((tt, GDN_DIM), row),
        out_shape=jax.ShapeDtypeStruct((m, GDN_DIM), BF16),
        scratch_shapes=[pltpu.VMEM((tt + 2 * SUBLANES, GDN_QKV_DIM), F32),
                        pltpu.VMEM((GDN_HEADS, GDN_HEAD_DIM, GDN_HEAD_DIM), F32)],
        compiler_params=pltpu.CompilerParams(dimension_semantics=("parallel", "arbitrary"),
                                             vmem_limit_bytes=VMEM_LIMIT),
        name="gated_deltanet",
    )(g, z, ba, conv_w.astype(F32), gate_params, norm_w.reshape(1, GDN_HEAD_DIM).astype(F32))


def _mix_ffn_kernel(x_ref, attn_ref, gdn_ref, wout_ref, nw_ref, win_ref, cw_ref, cb_ref, wd_ref, fnw_ref,
                    o_ref, gbuf_ref, carry_ref, act_ref, *, final):
    tm = x_ref.shape[0]
    halo = SUBLANES

    @pl.when(pl.program_id(1) == 0)
    def _():
        carry_ref[...] = jnp.zeros_like(carry_ref)

    dot = functools.partial(jnp.dot, preferred_element_type=F32)
    x1 = (x_ref[...] + dot(attn_ref[...], wout_ref[0:ATTN_Q_DIM, :])
          + dot(gdn_ref[...], wout_ref[ATTN_Q_DIM:MIX_DIM, :]))
    h = _rmsnorm(x1, nw_ref[...]).astype(BF16)

    for c in range(D_FF // FFN_FC):
        c0, c1 = c * FFN_FC, (c + 1) * FFN_FC
        gate = dot(h, win_ref[:, c0:c1])
        up = dot(h, win_ref[:, D_FF + c0:D_FF + c1])
        gbuf_ref[0:halo, :] = carry_ref[:, c0:c1]
        gbuf_ref[halo:halo + tm, :] = gate
        carry_ref[:, c0:c1] = gate[tm - halo:tm, :]
        conv = gate * cw_ref[2:3, c0:c1] + cb_ref[:, c0:c1]
        conv = conv + gbuf_ref[halo - 1:halo - 1 + tm, :] * cw_ref[1:2, c0:c1]
        conv = conv + gbuf_ref[halo - 2:halo - 2 + tm, :] * cw_ref[0:1, c0:c1]
        act_ref[:, c0:c1] = (_silu(conv) * up).astype(BF16)

    out = x1 + dot(act_ref[...], wd_ref[...])
    if final:
        out = _rmsnorm(out, fnw_ref[...])
    o_ref[...] = out


def _mix_ffn(x, attn, gdn, w_out, norm_w, w_ffn_in, conv_w, conv_b, w_down, final_norm, batch, seq, final):
    m = x.shape[0]
    tm = min(FFN_TM, seq)
    nt = seq // tm
    row = lambda b, t: (b * nt + t, 0)
    fixed = lambda b, t: (0, 0)
    resident = functools.partial(pl.BlockSpec, index_map=fixed, pipeline_mode=pl.Buffered(1))
    return pl.pallas_call(
        functools.partial(_mix_ffn_kernel, final=final),
        grid=(batch, nt),
        in_specs=[pl.BlockSpec((tm, D_MODEL), row),
                  pl.BlockSpec((tm, ATTN_Q_DIM), row),
                  pl.BlockSpec((tm, GDN_DIM), row),
                  resident((MIX_DIM, D_MODEL)),
                  pl.BlockSpec((1, D_MODEL), fixed),
                  resident((D_MODEL, 2 * D_FF)),
                  pl.BlockSpec((FFN_CONV, D_FF), fixed),
                  pl.BlockSpec((1, D_FF), fixed),
                  resident((D_FF, D_MODEL)),
                  pl.BlockSpec((1, D_MODEL), fixed)],
        out_specs=pl.BlockSpec((tm, D_MODEL), row),
        out_shape=jax.ShapeDtypeStruct((m, D_MODEL), F32),
        scratch_shapes=[pltpu.VMEM((tm + SUBLANES, FFN_FC), F32),
                        pltpu.VMEM((SUBLANES, D_FF), F32),
                        pltpu.VMEM((tm, D_FF), BF16)],
        compiler_params=pltpu.CompilerParams(dimension_semantics=("parallel", "arbitrary"),
                                             vmem_limit_bytes=VMEM_LIMIT),
        name="mix_ffn",
    )(x, attn, gdn, w_out, norm_w.reshape(1, D_MODEL), w_ffn_in, conv_w, conv_b.reshape(1, D_FF), w_down,
      final_norm.reshape(1, D_MODEL))


def kernel(x, attn_norm, w_in, attn_sinks, gdn_conv_w, gdn_a_log, gdn_dt_bias, gdn_norm, w_out, ffn_norm,
           w_ffn_in, ffn_conv_w, ffn_conv_b, w_down, final_norm):
    batch, seq, _ = x.shape
    assert seq % WINDOW == 0 and seq % GDN_CHUNK == 0
    xf = x.reshape(batch * seq, D_MODEL).astype(F32)
    bias = _attn_bias()
    depth = w_in.shape[0]
    for l in range(depth):
        w_pad = jnp.pad(w_in[l], ((0, 0), (0, IN_PAD - IN_DIM))).astype(BF16)
        qkv, g, z, ba = _inproj(xf, attn_norm[l].astype(F32), w_pad)
        attn = _attention(qkv, attn_sinks[l].astype(F32), bias, batch, seq)
        gdn = _gdn(g, z, ba, gdn_conv_w[l], gdn_a_log[l], gdn_dt_bias[l], gdn_norm[l], batch, seq)
        xf = _mix_ffn(xf, attn, gdn, w_out[l].astype(BF16), ffn_norm[l].astype(F32), w_ffn_in[l].astype(BF16),
                      ffn_conv_w[l].astype(F32), ffn_conv_b[l].astype(F32), w_down[l].astype(BF16),
                      final_norm.astype(F32), batch, seq, final=(l == depth - 1))
    return xf.reshape(batch, seq, D_MODEL).astype(x.dtype)
```

```python
import functools

import numpy as np
import jax
import jax.numpy as jnp
from jax import lax
from jax.experimental import pallas as pl
from jax.experimental.pallas import tpu as pltpu

D_MODEL = 1024
DEPTH = 2
ATTN_HEADS = 8
ATTN_KV_HEADS = 2
ATTN_GROUP = ATTN_HEADS // ATTN_KV_HEADS
ATTN_HEAD_DIM = 64
WINDOW = 128
GDN_HEADS = 4
GDN_HEAD_DIM = 128
GDN_CHUNK = 64
GDN_CONV = 4
D_FF = 2816
FFN_CONV = 3
NORM_EPS = 1e-6

ATTN_Q_DIM = ATTN_HEADS * ATTN_HEAD_DIM
ATTN_KV_DIM = ATTN_KV_HEADS * ATTN_HEAD_DIM
ATTN_QKV_DIM = ATTN_Q_DIM + 2 * ATTN_KV_DIM
GDN_DIM = GDN_HEADS * GDN_HEAD_DIM
GDN_QKV_DIM = 3 * GDN_DIM
MIX_DIM = ATTN_Q_DIM + GDN_DIM
IN_DIM = ATTN_QKV_DIM + GDN_QKV_DIM + GDN_DIM + 2 * GDN_HEADS
LANES = 128
SUBLANES = 8
GATE_DIM = LANES
IN_PAD = IN_DIM - 2 * GDN_HEADS + GATE_DIM

IN_TM = 512
GDN_TT = 256
FFN_TM = 512
FFN_FC = 256
VMEM_LIMIT = 56 * 1024 * 1024

NEG_BIG = -1e30
BF16 = jnp.bfloat16
F32 = jnp.float32


def _mm(a, b):
    return jnp.dot(a.astype(BF16), b.astype(BF16), preferred_element_type=F32)


def _mm_nt(a, b):
    return lax.dot_general(a.astype(BF16), b.astype(BF16), (((1,), (1,)), ((), ())),
                           preferred_element_type=F32)


def _mm_tn(a, b):
    return lax.dot_general(a.astype(BF16), b.astype(BF16), (((0,), (0,)), ((), ())),
                           preferred_element_type=F32)


def _split3(a):
    a1 = a.astype(BF16)
    r1 = a - a1.astype(F32)
    a2 = r1.astype(BF16)
    a3 = (r1 - a2.astype(F32)).astype(BF16)
    return a1, a2, a3


def _mm_hp(a, b):
    a1, a2, a3 = _split3(a)
    b1, b2, b3 = _split3(b)
    dot = functools.partial(jnp.dot, preferred_element_type=F32)
    small = dot(a2, b2) + dot(a1, b3) + dot(a3, b1)
    return (dot(a1, b2) + dot(a2, b1) + small) + dot(a1, b1)


def _rmsnorm(x, w):
    return x * lax.rsqrt(jnp.mean(x * x, axis=-1, keepdims=True) + NORM_EPS) * w


def _silu(x):
    return x / (1.0 + jnp.exp(-x))


def _inproj_kernel(x_ref, nw_ref, w_ref, qkv_ref, g_ref, z_ref, ba_ref):
    h = _rmsnorm(x_ref[...], nw_ref[...]).astype(BF16)
    o0, o1, o2, o3 = 0, ATTN_QKV_DIM, ATTN_QKV_DIM + GDN_QKV_DIM, IN_PAD - GATE_DIM
    qkv_ref[...] = jnp.dot(h, w_ref[:, o0:o1], preferred_element_type=F32).astype(BF16)
    g_ref[...] = jnp.dot(h, w_ref[:, o1:o2], preferred_element_type=F32).astype(BF16)
    z_ref[...] = jnp.dot(h, w_ref[:, o2:o3], preferred_element_type=F32).astype(BF16)
    ba_ref[...] = jnp.dot(h, w_ref[:, o3:IN_PAD], preferred_element_type=F32)


def _inproj(x, norm_w, w_pad):
    m = x.shape[0]
    tm = min(IN_TM, m)
    row = lambda i: (i, 0)
    fixed = lambda i: (0, 0)
    return pl.pallas_call(
        _inproj_kernel,
        grid=(m // tm,),
        in_specs=[pl.BlockSpec((tm, D_MODEL), row),
                  pl.BlockSpec((1, D_MODEL), fixed),
                  pl.BlockSpec((D_MODEL, IN_PAD), fixed)],
        out_specs=[pl.BlockSpec((tm, ATTN_QKV_DIM), row),
                   pl.BlockSpec((tm, GDN_QKV_DIM), row),
                   pl.BlockSpec((tm, GDN_DIM), row),
                   pl.BlockSpec((tm, GATE_DIM), row)],
        out_shape=[jax.ShapeDtypeStruct((m, ATTN_QKV_DIM), BF16),
                   jax.ShapeDtypeStruct((m, GDN_QKV_DIM), BF16),
                   jax.ShapeDtypeStruct((m, GDN_DIM), BF16),
                   jax.ShapeDtypeStruct((m, GATE_DIM), F32)],
        compiler_params=pltpu.CompilerParams(dimension_semantics=("parallel",),
                                             vmem_limit_bytes=VMEM_LIMIT),
        name="inproj",
    )(x, norm_w.reshape(1, D_MODEL), w_pad)


def _attn_bias():
    qpos = np.arange(WINDOW)[:, None] + WINDOW
    kpos = np.arange(2 * WINDOW)[None, :]
    rel = qpos - kpos
    band = (rel >= 0) & (rel < WINDOW)
    slopes = 2.0 ** (-8.0 * np.arange(1, ATTN_HEADS + 1) / ATTN_HEADS)
    alibi = -slopes[:, None, None] * rel[None].astype(np.float64)
    rest = np.where(band[None], alibi, NEG_BIG)
    first = np.where((band & (kpos >= WINDOW))[None], alibi, NEG_BIG)
    return jnp.asarray(np.stack([first, rest]), dtype=F32)


def _attn_kernel(sink_ref, q_ref, kvc_ref, kvp_ref, bias_ref, o_ref):
    dh = ATTN_HEAD_DIM
    for h in range(ATTN_KV_HEADS):
        k = jnp.concatenate([kvp_ref[:, h * dh:(h + 1) * dh], kvc_ref[:, h * dh:(h + 1) * dh]], axis=0)
        vo = ATTN_KV_DIM + h * dh
        v = jnp.concatenate([kvp_ref[:, vo:vo + dh], kvc_ref[:, vo:vo + dh]], axis=0)
        for g in range(ATTN_GROUP):
            hq = h * ATTN_GROUP + g
            q = q_ref[:, hq * dh:(hq + 1) * dh]
            s = _mm_nt(q, k) * (dh ** -0.5) + bias_ref[hq]
            sink = sink_ref[hq]
            m = jnp.maximum(jnp.max(s, axis=-1, keepdims=True), sink)
            p = jnp.exp(s - m)
            denom = jnp.sum(p, axis=-1, keepdims=True) + jnp.exp(sink - m)
            o = _mm(p, v) / denom
            o_ref[:, hq * dh:(hq + 1) * dh] = o.astype(BF16)


def _attention(qkv, sinks, bias, batch, seq):
    m = qkv.shape[0]
    nb = seq // WINDOW
    kv_col = ATTN_Q_DIM // (2 * ATTN_KV_DIM)
    return pl.pallas_call(
        _attn_kernel,
        grid=(batch, nb),
        in_specs=[pl.BlockSpec(memory_space=pltpu.SMEM),
                  pl.BlockSpec((WINDOW, ATTN_Q_DIM), lambda b, i: (b * nb + i, 0)),
                  pl.BlockSpec((WINDOW, 2 * ATTN_KV_DIM), lambda b, i: (b * nb + i, kv_col)),
                  pl.BlockSpec((WINDOW, 2 * ATTN_KV_DIM), lambda b, i: (b * nb + jnp.maximum(i - 1, 0), kv_col)),
                  pl.BlockSpec((None, ATTN_HEADS, WINDOW, 2 * WINDOW), lambda b, i: (jnp.minimum(i, 1), 0, 0, 0))],
        out_specs=pl.BlockSpec((WINDOW, ATTN_Q_DIM), lambda b, i: (b * nb + i, 0)),
        out_shape=jax.ShapeDtypeStruct((m, ATTN_Q_DIM), BF16),
        compiler_params=pltpu.CompilerParams(dimension_semantics=("parallel", "arbitrary"),
                                             vmem_limit_bytes=VMEM_LIMIT),
        name="swa_attention",
    )(sinks, qkv, qkv, qkv, bias)


def _gdn_kernel(g_ref, z_ref, ba_ref, cw_ref, gp_ref, nw_ref, o_ref, xbuf_ref, state_ref):
    tt = g_ref.shape[0]
    nchunk = tt // GDN_CHUNK
    hd = GDN_HEAD_DIM
    halo = SUBLANES
    heads = range(GDN_HEADS)

    @pl.when(pl.program_id(1) == 0)
    def _():
        state_ref[...] = jnp.zeros_like(state_ref)
        xbuf_ref[0:halo, :] = jnp.zeros((halo, GDN_QKV_DIM), F32)

    xbuf_ref[halo:halo + tt, :] = g_ref[...].astype(F32)

    def conv_silu(col):
        acc = None
        for j in range(GDN_CONV):
            sh = GDN_CONV - 1 - j
            term = xbuf_ref[halo - sh:halo - sh + tt, col:col + hd] * cw_ref[j:j + 1, col:col + hd]
            acc = term if acc is None else acc + term
        return _silu(acc)

    def l2norm(x):
        return x * lax.rsqrt(jnp.sum(x * x, axis=-1, keepdims=True) + NORM_EPS)

    ba = ba_ref[...]
    beta_all = 1.0 / (1.0 + jnp.exp(-ba))
    pre = ba + gp_ref[1:2, :]
    softplus = jnp.maximum(pre, 0.0) + jnp.log(1.0 + jnp.exp(-jnp.abs(pre)))
    g_all = -jnp.exp(gp_ref[0:1, :]) * softplus

    ri = lax.broadcasted_iota(jnp.int32, (tt, tt), 0)
    ci = lax.broadcasted_iota(jnp.int32, (tt, tt), 1)
    same_chunk = (ri // GDN_CHUNK) == (ci // GDN_CHUNK)
    causal = same_chunk & (ri >= ci)

    ltri = jnp.where(causal, 1.0, 0.0).astype(BF16)
    g1, g2, g3 = _split3(g_all)
    dot = functools.partial(jnp.dot, preferred_element_type=F32)
    gc_all = (dot(ltri, g3) + dot(ltri, g2)) + dot(ltri, g1)
    gc_rows = gc_all.T
    gcl_all = jnp.concatenate(
        [jnp.broadcast_to(gc_all[(c + 1) * GDN_CHUNK - 1:(c + 1) * GDN_CHUNK, :], (GDN_CHUNK, GATE_DIM))
         for c in range(nchunk)], axis=0)
    egc_all = jnp.exp(gc_all)
    ekd_all = jnp.exp(gcl_all - gc_all)

    level_masks = []
    s = 1
    while s < GDN_CHUNK:
        lower_left = ((ri // (2 * s)) == (ci // (2 * s))) & ((ri // s) > (ci // s))
        level_masks.append(jnp.where(lower_left, 1.0, 0.0).astype(BF16))
        s *= 2

    a_lv, qk_d, qd, kd, rhs = [], [], [], [], []
    for h in heads:
        q = l2norm(conv_silu(h * hd)) * (hd ** -0.5)
        k = l2norm(conv_silu(GDN_DIM + h * hd))
        v = conv_silu(2 * GDN_DIM + h * hd)
        beta = beta_all[:, h:h + 1]
        gc = gc_all[:, GDN_HEADS + h:GDN_HEADS + h + 1]
        gc_row = gc_rows[GDN_HEADS + h:GDN_HEADS + h + 1, :]
        egc = egc_all[:, GDN_HEADS + h:GDN_HEADS + h + 1]
        decay = jnp.exp(jnp.where(causal, gc - gc_row, NEG_BIG))
        kb = k * beta
        k16 = k.astype(BF16)
        a_mat = (_mm_nt(kb, k16) * decay).astype(BF16)
        qk = _mm_nt(q, k16) * decay
        qk_d.append([qk[c * GDN_CHUNK:(c + 1) * GDN_CHUNK, c * GDN_CHUNK:(c + 1) * GDN_CHUNK].astype(BF16)
                     for c in range(nchunk)])
        a_lv.append([a_mat * m for m in level_masks])
        qd.append((q * egc).astype(BF16))
        kd.append((k * ekd_all[:, GDN_HEADS + h:GDN_HEADS + h + 1]).astype(BF16))
        rhs.append(jnp.concatenate([v * beta, kb * egc], axis=1).astype(BF16))

    eye = jnp.where(ri == ci, 1.0, 0.0)
    tinv = [eye - a_lv[h][0].astype(F32) for h in heads]
    for lv in range(1, len(level_masks)):
        for h in heads:
            t16 = tinv[h].astype(BF16)
            x = dot(a_lv[h][lv], t16).astype(BF16)
            tinv[h] = tinv[h] - dot(t16, x)
    sol = [dot(tinv[h].astype(BF16), rhs[h]) for h in heads]

    states = [state_ref[h] for h in heads]
    for c in range(nchunk):
        r0, r1 = c * GDN_CHUNK, (c + 1) * GDN_CHUNK
        for h in heads:
            u_c = sol[h][r0:r1, 0:hd]
            wq = jnp.concatenate([sol[h][r0:r1, hd:2 * hd].astype(BF16), qd[h][r0:r1]], axis=0)
            ws = dot(wq, states[h].astype(BF16))
            v_new = (u_c - ws[0:GDN_CHUNK]).astype(BF16)
            o = ws[GDN_CHUNK:2 * GDN_CHUNK] + dot(qk_d[h][c], v_new)
            gl = egc_all[r1 - 1:r1, GDN_HEADS + h:GDN_HEADS + h + 1]
            states[h] = states[h] * gl + lax.dot_general(kd[h][r0:r1], v_new, (((0,), (0,)), ((), ())),
                                                         preferred_element_type=F32)
            o = o * lax.rsqrt(jnp.mean(o * o, axis=-1, keepdims=True) + NORM_EPS) * nw_ref[...]
            zz = z_ref[r0:r1, h * hd:(h + 1) * hd].astype(F32)
            o_ref[r0:r1, h * hd:(h + 1) * hd] = (o * _silu(zz)).astype(BF16)
    for h in heads:
        state_ref[h] = states[h]

    xbuf_ref[0:halo, :] = xbuf_ref[tt:tt + halo, :]


def _gdn(g, z, ba, conv_w, a_log, dt_bias, norm_w, batch, seq):
    m = g.shape[0]
    tt = min(GDN_TT, seq)
    nt = seq // tt
    gate_params = jnp.zeros((SUBLANES, GATE_DIM), F32)
    gate_params = gate_params.at[0, GDN_HEADS:2 * GDN_HEADS].set(a_log.astype(F32))
    gate_params = gate_params.at[1, GDN_HEADS:2 * GDN_HEADS].set(dt_bias.astype(F32))
    row = lambda b, t: (b * nt + t, 0)
    fixed = lambda b, t: (0, 0)
    return pl.pallas_call(
        _gdn_kernel,
        grid=(batch, nt),
        in_specs=[pl.BlockSpec((tt, GDN_QKV_DIM), row),
                  pl.BlockSpec((tt, GDN_DIM), row),
                  pl.BlockSpec((tt, GATE_DIM), row),
                  pl.BlockSpec((GDN_CONV, GDN_QKV_DIM), fixed),
                  pl.BlockSpec((SUBLANES, GATE_DIM), fixed),
                  pl.BlockSpec((1, GDN_HEAD_DIM), fixed)],
        out_specs=pl.BlockSpec((tt, GDN_DIM), row),
        out_shape=jax.ShapeDtypeStruct((m, GDN_DIM), BF16),
        scratch_shapes=[pltpu.VMEM((tt + SUBLANES, GDN_QKV_DIM), F32),
                        pltpu.VMEM((GDN_HEADS, GDN_HEAD_DIM, GDN_HEAD_DIM), F32)],
        compiler_params=pltpu.CompilerParams(dimension_semantics=("parallel", "arbitrary"),
                                             vmem_limit_bytes=VMEM_LIMIT),
        name="gated_deltanet",
    )(g, z, ba, conv_w.astype(F32), gate_params, norm_w.reshape(1, GDN_HEAD_DIM).astype(F32))


def _mix_ffn_kernel(x_ref, attn_ref, gdn_ref, wout_ref, nw_ref, win_ref, cw_ref, cb_ref, wd_ref, fnw_ref,
                    o_ref, gbuf_ref, carry_ref, act_ref, *, final):
    tm = x_ref.shape[0]
    halo = SUBLANES

    @pl.when(pl.program_id(1) == 0)
    def _():
        carry_ref[...] = jnp.zeros_like(carry_ref)

    dot = functools.partial(jnp.dot, preferred_element_type=F32)
    x1 = (x_ref[...] + dot(attn_ref[...], wout_ref[0:ATTN_Q_DIM, :])
          + dot(gdn_ref[...], wout_ref[ATTN_Q_DIM:MIX_DIM, :]))
    h = _rmsnorm(x1, nw_ref[...]).astype(BF16)

    for c in range(D_FF // FFN_FC):
        c0, c1 = c * FFN_FC, (c + 1) * FFN_FC
        gate = dot(h, win_ref[:, c0:c1])
        up = dot(h, win_ref[:, D_FF + c0:D_FF + c1])
        gbuf_ref[0:halo, :] = carry_ref[:, c0:c1]
        gbuf_ref[halo:halo + tm, :] = gate
        carry_ref[:, c0:c1] = gate[tm - halo:tm, :]
        conv = gate * cw_ref[2:3, c0:c1] + cb_ref[:, c0:c1]
        conv = conv + gbuf_ref[halo - 1:halo - 1 + tm, :] * cw_ref[1:2, c0:c1]
        conv = conv + gbuf_ref[halo - 2:halo - 2 + tm, :] * cw_ref[0:1, c0:c1]
        act_ref[:, c0:c1] = (_silu(conv) * up).astype(BF16)

    out = x1 + dot(act_ref[...], wd_ref[...])
    if final:
        out = _rmsnorm(out, fnw_ref[...])
    o_ref[...] = out


def _mix_ffn(x, attn, gdn, w_out, norm_w, w_ffn_in, conv_w, conv_b, w_down, final_norm, batch, seq, final):
    m = x.shape[0]
    tm = min(FFN_TM, seq)
    nt = seq // tm
    row = lambda b, t: (b * nt + t, 0)
    fixed = lambda b, t: (0, 0)
    resident = functools.partial(pl.BlockSpec, index_map=fixed, pipeline_mode=pl.Buffered(1))
    return pl.pallas_call(
        functools.partial(_mix_ffn_kernel, final=final),
        grid=(batch, nt),
        in_specs=[pl.BlockSpec((tm, D_MODEL), row),
                  pl.BlockSpec((tm, ATTN_Q_DIM), row),
                  pl.BlockSpec((tm, GDN_DIM), row),
                  resident((MIX_DIM, D_MODEL)),
                  pl.BlockSpec((1, D_MODEL), fixed),
                  resident((D_MODEL, 2 * D_FF)),
                  pl.BlockSpec((FFN_CONV, D_FF), fixed),
                  pl.BlockSpec((1, D_FF), fixed),
                  resident((D_FF, D_MODEL)),
                  pl.BlockSpec((1, D_MODEL), fixed)],
        out_specs=pl.BlockSpec((tm, D_MODEL), row),
        out_shape=jax.ShapeDtypeStruct((m, D_MODEL), F32),
        scratch_shapes=[pltpu.VMEM((tm + SUBLANES, FFN_FC), F32),
                        pltpu.VMEM((SUBLANES, D_FF), F32),
                        pltpu.VMEM((tm, D_FF), BF16)],
        compiler_params=pltpu.CompilerParams(dimension_semantics=("parallel", "arbitrary"),
                                             vmem_limit_bytes=VMEM_LIMIT),
        name="mix_ffn",
    )(x, attn, gdn, w_out, norm_w.reshape(1, D_MODEL), w_ffn_in, conv_w, conv_b.reshape(1, D_FF), w_down,
      final_norm.reshape(1, D_MODEL))


def kernel(x, attn_norm, w_in, attn_sinks, gdn_conv_w, gdn_a_log, gdn_dt_bias, gdn_norm, w_out, ffn_norm,
           w_ffn_in, ffn_conv_w, ffn_conv_b, w_down, final_norm):
    batch, seq, _ = x.shape
    assert seq % WINDOW == 0 and seq % GDN_CHUNK == 0
    xf = x.reshape(batch * seq, D_MODEL).astype(F32)
    bias = _attn_bias()
    depth = w_in.shape[0]
    for l in range(depth):
        w_pad = jnp.pad(w_in[l], ((0, 0), (0, IN_PAD - IN_DIM))).astype(BF16)
        qkv, g, z, ba = _inproj(xf, attn_norm[l].astype(F32), w_pad)
        attn = _attention(qkv, attn_sinks[l].astype(F32), bias, batch, seq)
        gdn = _gdn(g, z, ba, gdn_conv_w[l], gdn_a_log[l], gdn_dt_bias[l], gdn_norm[l], batch, seq)
        xf = _mix_ffn(xf, attn, gdn, w_out[l].astype(BF16), ffn_norm[l].astype(F32), w_ffn_in[l].astype(BF16),
                      ffn_conv_w[l].astype(F32), ffn_conv_b[l].astype(F32), w_down[l].astype(BF16),
                      final_norm.astype(F32), batch, seq, final=(l == depth - 1))
    return xf.reshape(batch, seq, D_MODEL).astype(x.dtype)
```

```python
import functools

import numpy as np
import jax
import jax.numpy as jnp
from jax import lax
from jax.experimental import pallas as pl
from jax.experimental.pallas import tpu as pltpu

D_MODEL = 1024
DEPTH = 2
ATTN_HEADS = 8
ATTN_KV_HEADS = 2
ATTN_GROUP = ATTN_HEADS // ATTN_KV_HEADS
ATTN_HEAD_DIM = 64
WINDOW = 128
GDN_HEADS = 4
GDN_HEAD_DIM = 128
GDN_CHUNK = 64
GDN_CONV = 4
D_FF = 2816
FFN_CONV = 3
NORM_EPS = 1e-6

ATTN_Q_DIM = ATTN_HEADS * ATTN_HEAD_DIM
ATTN_KV_DIM = ATTN_KV_HEADS * ATTN_HEAD_DIM
ATTN_QKV_DIM = ATTN_Q_DIM + 2 * ATTN_KV_DIM
GDN_DIM = GDN_HEADS * GDN_HEAD_DIM
GDN_QKV_DIM = 3 * GDN_DIM
MIX_DIM = ATTN_Q_DIM + GDN_DIM
IN_DIM = ATTN_QKV_DIM + GDN_QKV_DIM + GDN_DIM + 2 * GDN_HEADS
LANES = 128
SUBLANES = 8
GATE_DIM = LANES
IN_PAD = IN_DIM - 2 * GDN_HEADS + GATE_DIM

IN_TM = 512
ATTN_TQ = 512
GDN_TT = 256
FFN_TM = 512
FFN_FC = 256
VMEM_LIMIT = 56 * 1024 * 1024

NEG_BIG = -1e30
BF16 = jnp.bfloat16
F32 = jnp.float32


def _mm(a, b):
    return jnp.dot(a.astype(BF16), b.astype(BF16), preferred_element_type=F32)


def _mm_nt(a, b):
    return lax.dot_general(a.astype(BF16), b.astype(BF16), (((1,), (1,)), ((), ())),
                           preferred_element_type=F32)


def _mm_tn(a, b):
    return lax.dot_general(a.astype(BF16), b.astype(BF16), (((0,), (0,)), ((), ())),
                           preferred_element_type=F32)


def _split3(a):
    a1 = a.astype(BF16)
    r1 = a - a1.astype(F32)
    a2 = r1.astype(BF16)
    a3 = (r1 - a2.astype(F32)).astype(BF16)
    return a1, a2, a3


def _mm_hp(a, b):
    a1, a2, a3 = _split3(a)
    b1, b2, b3 = _split3(b)
    dot = functools.partial(jnp.dot, preferred_element_type=F32)
    small = dot(a2, b2) + dot(a1, b3) + dot(a3, b1)
    return (dot(a1, b2) + dot(a2, b1) + small) + dot(a1, b1)


def _rmsnorm(x, w):
    return x * lax.rsqrt(jnp.mean(x * x, axis=-1, keepdims=True) + NORM_EPS) * w


def _silu(x):
    return x / (1.0 + jnp.exp(-x))


def _shift_rows(x, carry_row):
    rolled = pltpu.roll(x, 1, axis=0)
    first = lax.broadcasted_iota(jnp.int32, x.shape, 0) == 0
    return jnp.where(first, carry_row, rolled)


def _inproj_kernel(x_ref, nw_ref, w_ref, cw_ref, qkv_ref, g_ref, z_ref, ba_ref, carry_ref):
    tm = x_ref.shape[0]
    hd = GDN_HEAD_DIM
    o0, o1, o2, o3 = 0, ATTN_QKV_DIM, ATTN_QKV_DIM + GDN_QKV_DIM, IN_PAD - GATE_DIM
    dot = functools.partial(jnp.dot, preferred_element_type=F32)

    @pl.when(pl.program_id(1) == 0)
    def _():
        carry_ref[...] = jnp.zeros_like(carry_ref)

    h = _rmsnorm(x_ref[...], nw_ref[...]).astype(BF16)
    graw = dot(h, w_ref[:, o1:o2])
    qkv_ref[...] = dot(h, w_ref[:, o0:o1]).astype(BF16)
    z_ref[...] = dot(h, w_ref[:, o2:o3]).astype(BF16)
    ba_ref[...] = dot(h, w_ref[:, o3:IN_PAD])

    last = GDN_CONV - 1
    for blk in range(GDN_QKV_DIM // hd):
        c0, c1 = blk * hd, (blk + 1) * hd
        x = graw[:, c0:c1]
        tail = carry_ref[SUBLANES - last:SUBLANES, c0:c1]
        acc = x * cw_ref[0:1, c0:c1]
        pref = [tail[i:i + 1, :] * cw_ref[0:1, c0:c1] for i in range(last)]
        for j in range(1, GDN_CONV):
            acc = x * cw_ref[j:j + 1, c0:c1] + _shift_rows(acc, pref[last - 1])
            pref = [tail[i:i + 1, :] * cw_ref[j:j + 1, c0:c1] + (pref[i - 1] if i > 0 else 0.0)
                    for i in range(last)]
        carry_ref[:, c0:c1] = x[tm - SUBLANES:tm, :]
        y = _silu(acc)
        part = blk // GDN_HEADS
        if part < 2:
            y = y * lax.rsqrt(jnp.sum(y * y, axis=-1, keepdims=True) + NORM_EPS)
        if part == 0:
            y = y * (hd ** -0.5)
        g_ref[:, c0:c1] = y.astype(BF16)


def _inproj(x, norm_w, w_pad, conv_w, batch, seq):
    m = x.shape[0]
    tm = min(IN_TM, seq)
    nt = seq // tm
    row = lambda b, t: (b * nt + t, 0)
    fixed = lambda b, t: (0, 0)
    return pl.pallas_call(
        _inproj_kernel,
        grid=(batch, nt),
        in_specs=[pl.BlockSpec((tm, D_MODEL), row),
                  pl.BlockSpec((1, D_MODEL), fixed),
                  pl.BlockSpec((D_MODEL, IN_PAD), fixed, pipeline_mode=pl.Buffered(1)),
                  pl.BlockSpec((GDN_CONV, GDN_QKV_DIM), fixed)],
        out_specs=[pl.BlockSpec((tm, ATTN_QKV_DIM), row),
                   pl.BlockSpec((tm, GDN_QKV_DIM), row),
                   pl.BlockSpec((tm, GDN_DIM), row),
                   pl.BlockSpec((tm, GATE_DIM), row)],
        out_shape=[jax.ShapeDtypeStruct((m, ATTN_QKV_DIM), BF16),
                   jax.ShapeDtypeStruct((m, GDN_QKV_DIM), BF16),
                   jax.ShapeDtypeStruct((m, GDN_DIM), BF16),
                   jax.ShapeDtypeStruct((m, GATE_DIM), F32)],
        scratch_shapes=[pltpu.VMEM((SUBLANES, GDN_QKV_DIM), F32)],
        compiler_params=pltpu.CompilerParams(dimension_semantics=("parallel", "arbitrary"),
                                             vmem_limit_bytes=VMEM_LIMIT),
        name="inproj",
    )(x, norm_w.reshape(1, D_MODEL), w_pad, conv_w)


def _attn_bias():
    qpos = np.arange(WINDOW)[None, :] + WINDOW
    kpos = np.arange(2 * WINDOW)[:, None]
    rel = qpos - kpos
    band = (rel >= 0) & (rel < WINDOW)
    slopes = 2.0 ** (-8.0 * np.arange(1, ATTN_HEADS + 1) / ATTN_HEADS)
    alibi = -slopes[:, None, None] * rel[None].astype(np.float64)
    rest = np.where(band[None], alibi, NEG_BIG)
    first = np.where((band & (kpos >= WINDOW))[None], alibi, NEG_BIG)
    return jnp.asarray(np.stack([first, rest]), dtype=F32)


def _attn_kernel(sink_ref, q_ref, kvc_ref, kvp_ref, bias_ref, o_ref, kv_ref):
    dh = ATTN_HEAD_DIM
    w = WINDOW
    nsub = q_ref.shape[0] // w
    nkv = kv_ref.shape[0]
    first_tile = pl.program_id(1) == 0
    kv_ref[0:w, :] = kvp_ref[...]
    kv_ref[w:nkv, :] = kvc_ref[...]
    ones = jnp.ones((nkv, dh), F32)
    vt_ext = []
    for h in range(ATTN_KV_HEADS):
        vo = ATTN_KV_DIM + h * dh
        v_ext = jnp.concatenate([kv_ref[:, vo:vo + dh].astype(F32), ones], axis=1)
        vt_ext.append(v_ext.T.astype(BF16))
    for j in range(nsub):
        r0 = j * w
        bias_idx = jnp.where(first_tile, 0, 1) if j == 0 else 1
        for h in range(ATTN_KV_HEADS):
            k = kv_ref[r0:r0 + 2 * w, h * dh:(h + 1) * dh]
            vt = vt_ext[h][:, r0:r0 + 2 * w]
            for gp in range(ATTN_GROUP // 2):
                hq0 = h * ATTN_GROUP + 2 * gp
                res = []
                for hq in (hq0, hq0 + 1):
                    q = q_ref[r0:r0 + w, hq * dh:(hq + 1) * dh] * (dh ** -0.5)
                    s = _mm_nt(k, q) + bias_ref[bias_idx, hq]
                    sink = sink_ref[hq]
                    m = jnp.maximum(jnp.max(s, axis=0, keepdims=True), sink)
                    p = jnp.exp(s - m).astype(BF16)
                    acc = jnp.dot(vt, p, preferred_element_type=F32)
                    den = acc[dh:dh + 1, :] + jnp.exp(sink - m)
                    res.append(acc[0:dh, :] / den)
                o_ref[r0:r0 + w, hq0 * dh:(hq0 + 2) * dh] = jnp.concatenate(res, axis=0).T.astype(BF16)


def _attention(qkv, sinks, bias, batch, seq):
    m = qkv.shape[0]
    tq = min(ATTN_TQ, seq)
    nt = seq // tq
    sub = tq // WINDOW
    nb = seq // WINDOW
    kv_col = ATTN_Q_DIM // (2 * ATTN_KV_DIM)
    return pl.pallas_call(
        _attn_kernel,
        grid=(batch, nt),
        in_specs=[pl.BlockSpec(memory_space=pltpu.SMEM),
                  pl.BlockSpec((tq, ATTN_Q_DIM), lambda b, i: (b * nt + i, 0)),
                  pl.BlockSpec((tq, 2 * ATTN_KV_DIM), lambda b, i: (b * nt + i, kv_col)),
                  pl.BlockSpec((WINDOW, 2 * ATTN_KV_DIM), lambda b, i: (b * nb + jnp.maximum(i * sub - 1, 0), kv_col)),
                  pl.BlockSpec((2, ATTN_HEADS, 2 * WINDOW, WINDOW), lambda b, i: (0, 0, 0, 0))],
        out_specs=pl.BlockSpec((tq, ATTN_Q_DIM), lambda b, i: (b * nt + i, 0)),
        out_shape=jax.ShapeDtypeStruct((m, ATTN_Q_DIM), BF16),
        scratch_shapes=[pltpu.VMEM((tq + WINDOW, 2 * ATTN_KV_DIM), BF16)],
        compiler_params=pltpu.CompilerParams(dimension_semantics=("parallel", "arbitrary"),
                                             vmem_limit_bytes=VMEM_LIMIT),
        name="swa_attention",
    )(sinks, qkv, qkv, qkv, bias)


def _gdn_kernel(g_ref, z_ref, ba_ref, gp_ref, nw_ref, o_ref, state_ref):
    tt = g_ref.shape[0]
    nchunk = tt // GDN_CHUNK
    hd = GDN_HEAD_DIM
    heads = range(GDN_HEADS)

    @pl.when(pl.program_id(1) == 0)
    def _():
        state_ref[...] = jnp.zeros_like(state_ref)

    ba = ba_ref[...]
    beta_all = 1.0 / (1.0 + jnp.exp(-ba))
    pre = ba + gp_ref[1:2, :]
    softplus = jnp.maximum(pre, 0.0) + jnp.log(1.0 + jnp.exp(-jnp.abs(pre)))
    g_all = -jnp.exp(gp_ref[0:1, :]) * softplus

    ri = lax.broadcasted_iota(jnp.int32, (tt, tt), 0)
    ci = lax.broadcasted_iota(jnp.int32, (tt, tt), 1)
    same_chunk = (ri // GDN_CHUNK) == (ci // GDN_CHUNK)
    causal = same_chunk & (ri >= ci)

    ltri = jnp.where(causal, 1.0, 0.0).astype(BF16)
    g1, g2, g3 = _split3(g_all)
    dot = functools.partial(jnp.dot, preferred_element_type=F32)
    gc_all = (dot(ltri, g3) + dot(ltri, g2)) + dot(ltri, g1)
    gc_rows = gc_all.T
    gcl_all = jnp.concatenate(
        [jnp.broadcast_to(gc_all[(c + 1) * GDN_CHUNK - 1:(c + 1) * GDN_CHUNK, :], (GDN_CHUNK, GATE_DIM))
         for c in range(nchunk)], axis=0)
    egc_all = jnp.exp(gc_all)
    ekd_all = jnp.exp(gcl_all - gc_all)

    level_masks = []
    s = 1
    while s < GDN_CHUNK:
        lower_left = ((ri // (2 * s)) == (ci // (2 * s))) & ((ri // s) > (ci // s))
        level_masks.append(jnp.where(lower_left, 1.0, 0.0).astype(BF16))
        s *= 2

    a_lv, qk_d, qd, kd, rhs = [], [], [], [], []
    for h in heads:
        q = g_ref[:, h * hd:(h + 1) * hd].astype(F32)
        k16 = g_ref[:, GDN_DIM + h * hd:GDN_DIM + (h + 1) * hd]
        k = k16.astype(F32)
        v = g_ref[:, 2 * GDN_DIM + h * hd:2 * GDN_DIM + (h + 1) * hd].astype(F32)
        beta = beta_all[:, h:h + 1]
        gc = gc_all[:, GDN_HEADS + h:GDN_HEADS + h + 1]
        gc_row = gc_rows[GDN_HEADS + h:GDN_HEADS + h + 1, :]
        egc = egc_all[:, GDN_HEADS + h:GDN_HEADS + h + 1]
        decay = jnp.exp(jnp.where(causal, gc - gc_row, NEG_BIG))
        kb = k * beta
        a_mat = (_mm_nt(kb, k16) * decay).astype(BF16)
        qk = _mm_nt(q, k16) * decay
        qk_d.append([qk[c * GDN_CHUNK:(c + 1) * GDN_CHUNK, c * GDN_CHUNK:(c + 1) * GDN_CHUNK].astype(BF16)
                     for c in range(nchunk)])
        a_lv.append([a_mat * m for m in level_masks])
        qd.append((q * egc).astype(BF16))
        kd.append((k * ekd_all[:, GDN_HEADS + h:GDN_HEADS + h + 1]).astype(BF16))
        rhs.append(jnp.concatenate([v * beta, kb * egc], axis=1).astype(BF16))

    eye = jnp.where(ri == ci, 1.0, 0.0)
    tinv = [eye - a_lv[h][0].astype(F32) for h in heads]
    for lv in range(1, len(level_masks)):
        for h in heads:
            t16 = tinv[h].astype(BF16)
            x = dot(a_lv[h][lv], t16).astype(BF16)
            tinv[h] = tinv[h] - dot(t16, x)
    sol = [dot(tinv[h].astype(BF16), rhs[h]) for h in heads]

    states = [state_ref[h] for h in heads]
    for c in range(nchunk):
        r0, r1 = c * GDN_CHUNK, (c + 1) * GDN_CHUNK
        for h in heads:
            u_c = sol[h][r0:r1, 0:hd]
            wq = jnp.concatenate([sol[h][r0:r1, hd:2 * hd].astype(BF16), qd[h][r0:r1]], axis=0)
            ws = dot(wq, states[h].astype(BF16))
            v_new = (u_c - ws[0:GDN_CHUNK]).astype(BF16)
            o = ws[GDN_CHUNK:2 * GDN_CHUNK] + dot(qk_d[h][c], v_new)
            gl = egc_all[r1 - 1:r1, GDN_HEADS + h:GDN_HEADS + h + 1]
            states[h] = states[h] * gl + lax.dot_general(kd[h][r0:r1], v_new, (((0,), (0,)), ((), ())),
                                                         preferred_element_type=F32)
            o = o * lax.rsqrt(jnp.mean(o * o, axis=-1, keepdims=True) + NORM_EPS) * nw_ref[...]
            zz = z_ref[r0:r1, h * hd:(h + 1) * hd].astype(F32)
            o_ref[r0:r1, h * hd:(h + 1) * hd] = (o * _silu(zz)).astype(BF16)
    for h in heads:
        state_ref[h] = states[h]


def _gdn(g, z, ba, a_log, dt_bias, norm_w, batch, seq):
    m = g.shape[0]
    tt = min(GDN_TT, seq)
    nt = seq // tt
    gate_params = jnp.zeros((SUBLANES, GATE_DIM), F32)
    gate_params = gate_params.at[0, GDN_HEADS:2 * GDN_HEADS].set(a_log.astype(F32))
    gate_params = gate_params.at[1, GDN_HEADS:2 * GDN_HEADS].set(dt_bias.astype(F32))
    row = lambda b, t: (b * nt + t, 0)
    fixed = lambda b, t: (0, 0)
    return pl.pallas_call(
        _gdn_kernel,
        grid=(batch, nt),
        in_specs=[pl.BlockSpec((tt, GDN_QKV_DIM), row),
                  pl.BlockSpec((tt, GDN_DIM), row),
                  pl.BlockSpec((tt, GATE_DIM), row),
                  pl.BlockSpec((SUBLANES, GATE_DIM), fixed),
                  pl.BlockSpec((1, GDN_HEAD_DIM), fixed)],
        out_specs=pl.BlockSpec((tt, GDN_DIM), row),
        out_shape=jax.ShapeDtypeStruct((m, GDN_DIM), BF16),
        scratch_shapes=[pltpu.VMEM((GDN_HEADS, GDN_HEAD_DIM, GDN_HEAD_DIM), F32)],
        compiler_params=pltpu.CompilerParams(dimension_semantics=("parallel", "arbitrary"),
                                             vmem_limit_bytes=VMEM_LIMIT),
        name="gated_deltanet",
    )(g, z, ba, gate_params, norm_w.reshape(1, GDN_HEAD_DIM).astype(F32))


def _mix_ffn_kernel(x_ref, attn_ref, gdn_ref, wout_ref, nw_ref, win_ref, cw_ref, cb_ref, wd_ref, fnw_ref,
                    o_ref, gbuf_ref, carry_ref, act_ref, *, final):
    tm = x_ref.shape[0]
    halo = SUBLANES

    @pl.when(pl.program_id(1) == 0)
    def _():
        carry_ref[...] = jnp.zeros_like(carry_ref)

    dot = functools.partial(jnp.dot, preferred_element_type=F32)
    x1 = (x_ref[...] + dot(attn_ref[...], wout_ref[0:ATTN_Q_DIM, :])
          + dot(gdn_ref[...], wout_ref[ATTN_Q_DIM:MIX_DIM, :]))
    h = _rmsnorm(x1, nw_ref[...]).astype(BF16)

    for c in range(D_FF // FFN_FC):
        c0, c1 = c * FFN_FC, (c + 1) * FFN_FC
        gate = dot(h, win_ref[:, c0:c1])
        up = dot(h, win_ref[:, D_FF + c0:D_FF + c1])
        gbuf_ref[0:halo, :] = carry_ref[:, c0:c1]
        gbuf_ref[halo:halo + tm, :] = gate
        carry_ref[:, c0:c1] = gate[tm - halo:tm, :]
        conv = gate * cw_ref[2:3, c0:c1] + cb_ref[:, c0:c1]
        conv = conv + gbuf_ref[halo - 1:halo - 1 + tm, :] * cw_ref[1:2, c0:c1]
        conv = conv + gbuf_ref[halo - 2:halo - 2 + tm, :] * cw_ref[0:1, c0:c1]
        act_ref[:, c0:c1] = (_silu(conv) * up).astype(BF16)

    out = x1 + dot(act_ref[...], wd_ref[...])
    if final:
        out = _rmsnorm(out, fnw_ref[...])
    o_ref[...] = out


def _mix_ffn(x, attn, gdn, w_out, norm_w, w_ffn_in, conv_w, conv_b, w_down, final_norm, batch, seq, final):
    m = x.shape[0]
    tm = min(FFN_TM, seq)
    nt = seq // tm
    row = lambda b, t: (b * nt + t, 0)
    fixed = lambda b, t: (0, 0)
    resident = functools.partial(pl.BlockSpec, index_map=fixed, pipeline_mode=pl.Buffered(1))
    return pl.pallas_call(
        functools.partial(_mix_ffn_kernel, final=final),
        grid=(batch, nt),
        in_specs=[pl.BlockSpec((tm, D_MODEL), row),
                  pl.BlockSpec((tm, ATTN_Q_DIM), row),
                  pl.BlockSpec((tm, GDN_DIM), row),
                  resident((MIX_DIM, D_MODEL)),
                  pl.BlockSpec((1, D_MODEL), fixed),
                  resident((D_MODEL, 2 * D_FF)),
                  pl.BlockSpec((FFN_CONV, D_FF), fixed),
                  pl.BlockSpec((1, D_FF), fixed),
                  resident((D_FF, D_MODEL)),
                  pl.BlockSpec((1, D_MODEL), fixed)],
        out_specs=pl.BlockSpec((tm, D_MODEL), row),
        out_shape=jax.ShapeDtypeStruct((m, D_MODEL), F32),
        scratch_shapes=[pltpu.VMEM((tm + SUBLANES, FFN_FC), F32),
                        pltpu.VMEM((SUBLANES, D_FF), F32),
                        pltpu.VMEM((tm, D_FF), BF16)],
        compiler_params=pltpu.CompilerParams(dimension_semantics=("parallel", "arbitrary"),
                                             vmem_limit_bytes=VMEM_LIMIT),
        name="mix_ffn",
    )(x, attn, gdn, w_out, norm_w.reshape(1, D_MODEL), w_ffn_in, conv_w, conv_b.reshape(1, D_FF), w_down,
      final_norm.reshape(1, D_MODEL))


def kernel(x, attn_norm, w_in, attn_sinks, gdn_conv_w, gdn_a_log, gdn_dt_bias, gdn_norm, w_out, ffn_norm,
           w_ffn_in, ffn_conv_w, ffn_conv_b, w_down, final_norm):
    batch, seq, _ = x.shape
    assert seq % WINDOW == 0 and seq % GDN_CHUNK == 0
    xf = x.reshape(batch * seq, D_MODEL).astype(F32)
    bias = _attn_bias()
    depth = w_in.shape[0]
    for l in range(depth):
        w_pad = jnp.pad(w_in[l], ((0, 0), (0, IN_PAD - IN_DIM))).astype(BF16)
        qkv, g, z, ba = _inproj(xf, attn_norm[l].astype(F32), w_pad, gdn_conv_w[l].astype(F32), batch, seq)
        attn = _attention(qkv, attn_sinks[l].astype(F32), bias, batch, seq)
        gdn = _gdn(g, z, ba, gdn_a_log[l], gdn_dt_bias[l], gdn_norm[l], batch, seq)
        xf = _mix_ffn(xf, attn, gdn, w_out[l].astype(BF16), ffn_norm[l].astype(F32), w_ffn_in[l].astype(BF16),
                      ffn_conv_w[l].astype(F32), ffn_conv_b[l].astype(F32), w_down[l].astype(BF16),
                      final_norm.astype(F32), batch, seq, final=(l == depth - 1))
    return xf.reshape(batch, seq, D_MODEL).astype(x.dtype)
```

```python
import functools

import numpy as np
import jax
import jax.numpy as jnp
from jax import lax
from jax.experimental import pallas as pl
from jax.experimental.pallas import tpu as pltpu

D_MODEL = 1024
ATTN_HEADS = 8
ATTN_KV_HEADS = 2
ATTN_GROUP = ATTN_HEADS // ATTN_KV_HEADS
ATTN_HEAD_DIM = 64
WINDOW = 128
GDN_HEADS = 4
GDN_HEAD_DIM = 128
GDN_CHUNK = 64
GDN_CONV = 4
D_FF = 2816
FFN_CONV = 3
NORM_EPS = 1e-6

ATTN_Q_DIM = ATTN_HEADS * ATTN_HEAD_DIM
ATTN_KV_DIM = ATTN_KV_HEADS * ATTN_HEAD_DIM
ATTN_QKV_DIM = ATTN_Q_DIM + 2 * ATTN_KV_DIM
GDN_DIM = GDN_HEADS * GDN_HEAD_DIM
GDN_QKV_DIM = 3 * GDN_DIM
MIX_DIM = ATTN_Q_DIM + GDN_DIM
IN_DIM = ATTN_QKV_DIM + GDN_QKV_DIM + GDN_DIM + 2 * GDN_HEADS
LANES = 128
SUBLANES = 8
GATE_DIM = LANES
IN_PAD = IN_DIM - 2 * GDN_HEADS + GATE_DIM

IN_TM = 512
MIX_TT = 256
FFN_TM = 512
FFN_FC = 256
VMEM_LIMIT = 56 * 1024 * 1024

NEG_BIG = -1e30
BF16 = jnp.bfloat16
F32 = jnp.float32


def _mm_nt(a, b):
    return lax.dot_general(a.astype(BF16), b.astype(BF16), (((1,), (1,)), ((), ())),
                           preferred_element_type=F32)


def _split3(a):
    a1 = a.astype(BF16)
    r1 = a - a1.astype(F32)
    a2 = r1.astype(BF16)
    a3 = (r1 - a2.astype(F32)).astype(BF16)
    return a1, a2, a3


def _mul_rows(x, w8):
    n, d = x.shape
    return (x.reshape(n // SUBLANES, SUBLANES, d) * w8[None]).reshape(n, d)


def _add_rows(x, w8):
    n, d = x.shape
    return (x.reshape(n // SUBLANES, SUBLANES, d) + w8[None]).reshape(n, d)


def _rows8(w):
    w = w.astype(F32)
    if w.ndim == 1:
        return jnp.broadcast_to(w[None, :], (SUBLANES, w.shape[0]))
    return jnp.broadcast_to(w[:, None, :], (w.shape[0], SUBLANES, w.shape[1]))


def _rmsnorm(x, w8):
    return _mul_rows(x * lax.rsqrt(jnp.mean(x * x, axis=-1, keepdims=True) + NORM_EPS), w8)


def _silu(x):
    return x / (1.0 + jnp.exp(-x))


def _inproj_kernel(x_ref, nw_ref, w_ref, cw_ref, qkv_ref, g_ref, z_ref, ba_ref, gbuf_ref):
    tm = x_ref.shape[0]
    hd = GDN_HEAD_DIM
    halo = SUBLANES
    o0, o1, o2, o3 = 0, ATTN_QKV_DIM, ATTN_QKV_DIM + GDN_QKV_DIM, IN_PAD - GATE_DIM
    dot = functools.partial(jnp.dot, preferred_element_type=F32)

    nblk = GDN_QKV_DIM // hd

    @pl.when(pl.program_id(1) == 0)
    def _():
        gbuf_ref[:, 0:halo, :] = jnp.zeros((nblk, halo, hd), F32)

    h = _rmsnorm(x_ref[...], nw_ref[...]).astype(BF16)
    graw = dot(h, w_ref[:, o1:o2])
    for blk in range(nblk):
        gbuf_ref[blk, halo:halo + tm, :] = graw[:, blk * hd:(blk + 1) * hd]
    qkv_ref[...] = dot(h, w_ref[:, o0:o1]).astype(BF16)
    z_ref[...] = dot(h, w_ref[:, o2:o3]).astype(BF16)
    ba_ref[...] = dot(h, w_ref[:, o3:IN_PAD])

    for blk in range(nblk):
        c0, c1 = blk * hd, (blk + 1) * hd
        acc = None
        for j in range(GDN_CONV):
            sh = GDN_CONV - 1 - j
            term = _mul_rows(gbuf_ref[blk, halo - sh:halo - sh + tm, :], cw_ref[j, :, c0:c1])
            acc = term if acc is None else acc + term
        y = _silu(acc)
        part = blk // GDN_HEADS
        if part < 2:
            y = y * lax.rsqrt(jnp.sum(y * y, axis=-1, keepdims=True) + NORM_EPS)
        if part == 0:
            y = y * (hd ** -0.5)
        g_ref[:, c0:c1] = y.astype(BF16)
    gbuf_ref[:, 0:halo, :] = gbuf_ref[:, tm:tm + halo, :]


def _inproj(x, norm_w, w_pad, conv_w, batch, seq):
    m = x.shape[0]
    tm = min(IN_TM, seq)
    nt = seq // tm
    row = lambda b, t: (b * nt + t, 0)
    fixed = lambda b, t: (0, 0)
    return pl.pallas_call(
        _inproj_kernel,
        grid=(batch, nt),
        in_specs=[pl.BlockSpec((tm, D_MODEL), row),
                  pl.BlockSpec((SUBLANES, D_MODEL), fixed),
                  pl.BlockSpec((D_MODEL, IN_PAD), fixed, pipeline_mode=pl.Buffered(1)),
                  pl.BlockSpec((GDN_CONV, SUBLANES, GDN_QKV_DIM), lambda b, t: (0, 0, 0))],
        out_specs=[pl.BlockSpec((tm, ATTN_QKV_DIM), row),
                   pl.BlockSpec((tm, GDN_QKV_DIM), row),
                   pl.BlockSpec((tm, GDN_DIM), row),
                   pl.BlockSpec((tm, GATE_DIM), row)],
        out_shape=[jax.ShapeDtypeStruct((m, ATTN_QKV_DIM), BF16),
                   jax.ShapeDtypeStruct((m, GDN_QKV_DIM), BF16),
                   jax.ShapeDtypeStruct((m, GDN_DIM), BF16),
                   jax.ShapeDtypeStruct((m, GATE_DIM), F32)],
        scratch_shapes=[pltpu.VMEM((GDN_QKV_DIM // GDN_HEAD_DIM, tm + SUBLANES, GDN_HEAD_DIM), F32)],
        compiler_params=pltpu.CompilerParams(dimension_semantics=("parallel", "arbitrary"),
                                             vmem_limit_bytes=VMEM_LIMIT),
        name="inproj",
    )(x, _rows8(norm_w), w_pad, _rows8(conv_w))


def _attn_bias():
    qpos = np.arange(WINDOW)[None, :] + WINDOW
    kpos = np.arange(2 * WINDOW)[:, None]
    rel = qpos - kpos
    band = (rel >= 0) & (rel < WINDOW)
    slopes = 2.0 ** (-8.0 * np.arange(1, ATTN_HEADS + 1) / ATTN_HEADS)
    alibi = -slopes[:, None, None] * rel[None].astype(np.float64)
    rest = np.where(band[None], alibi, NEG_BIG)
    first = np.where((band & (kpos >= WINDOW))[None], alibi, NEG_BIG)
    return jnp.asarray(np.stack([first, rest]), dtype=F32)


def _attn_steps(sink_ref, q_ref, kvc_ref, kvp_ref, bias_ref, o_ref, kv_ref, first_tile):
    dh = ATTN_HEAD_DIM
    w = WINDOW
    nsub = q_ref.shape[0] // w
    nkv = kv_ref.shape[0]
    kv_ref[0:w, :] = kvp_ref[...]
    kv_ref[w:nkv, :] = kvc_ref[...]
    ones = jnp.ones((nkv, dh), F32)
    vt_ext = []
    for h in range(ATTN_KV_HEADS):
        vo = ATTN_KV_DIM + h * dh
        v_ext = jnp.concatenate([kv_ref[:, vo:vo + dh].astype(F32), ones], axis=1)
        vt_ext.append(v_ext.T.astype(BF16))
    yield
    for j in range(nsub):
        r0 = j * w
        bias_idx = jnp.where(first_tile, 0, 1) if j == 0 else 1
        for h in range(ATTN_KV_HEADS):
            k = kv_ref[r0:r0 + 2 * w, h * dh:(h + 1) * dh]
            vt = vt_ext[h][:, r0:r0 + 2 * w]
            for gp in range(ATTN_GROUP // 2):
                hq0 = h * ATTN_GROUP + 2 * gp
                res = []
                for hq in (hq0, hq0 + 1):
                    q = q_ref[r0:r0 + w, hq * dh:(hq + 1) * dh] * (dh ** -0.5)
                    s = _mm_nt(k, q) + bias_ref[bias_idx, hq]
                    sink = sink_ref[hq]
                    m = jnp.maximum(jnp.max(s, axis=0, keepdims=True), sink)
                    p = jnp.exp(s - m).astype(BF16)
                    acc = jnp.dot(vt, p, preferred_element_type=F32)
                    den = acc[dh:dh + 1, :] + jnp.exp(sink - m)
                    res.append(acc[0:dh, :] / den)
                o_ref[r0:r0 + w, hq0 * dh:(hq0 + 2) * dh] = jnp.concatenate(res, axis=0).T.astype(BF16)
                yield


GDN_LEVELS = GDN_CHUNK.bit_length() - 1


def _gdn_chunk_local_steps(g_ref, ba_ref, gp_ref, staged):
    tt = g_ref.shape[0]
    nchunk = tt // GDN_CHUNK
    hd = GDN_HEAD_DIM
    heads = range(GDN_HEADS)
    dot = functools.partial(jnp.dot, preferred_element_type=F32)

    ba = ba_ref[...]
    beta_all = 1.0 / (1.0 + jnp.exp(-ba))
    pre = ba + gp_ref[1:2, :]
    softplus = jnp.maximum(pre, 0.0) + jnp.log(1.0 + jnp.exp(-jnp.abs(pre)))
    g_all = -jnp.exp(gp_ref[0:1, :]) * softplus

    ri = lax.broadcasted_iota(jnp.int32, (tt, tt), 0)
    ci = lax.broadcasted_iota(jnp.int32, (tt, tt), 1)
    same_chunk = (ri // GDN_CHUNK) == (ci // GDN_CHUNK)
    causal = same_chunk & (ri >= ci)

    ltri = jnp.where(causal, 1.0, 0.0).astype(BF16)
    g1, g2, g3 = _split3(g_all)
    gc_all = (dot(ltri, g3) + dot(ltri, g2)) + dot(ltri, g1)
    gc_rows = gc_all.T
    gcl_all = jnp.concatenate(
        [jnp.broadcast_to(gc_all[(c + 1) * GDN_CHUNK - 1:(c + 1) * GDN_CHUNK, :], (GDN_CHUNK, GATE_DIM))
         for c in range(nchunk)], axis=0)
    egc_all = jnp.exp(gc_all)
    ekd_all = jnp.exp(gcl_all - gc_all)

    level_masks = []
    for lv in range(GDN_LEVELS):
        s = 1 << lv
        lower_left = ((ri // (2 * s)) == (ci // (2 * s))) & ((ri // s) > (ci // s))
        level_masks.append(jnp.where(lower_left, 1.0, 0.0).astype(BF16))
    eye = jnp.where(ri == ci, 1.0, 0.0)
    yield

    a_lv, qk_d, qd, kd, rhs = [], [], [], [], []
    for h in heads:
        q = g_ref[:, h * hd:(h + 1) * hd].astype(F32)
        k16 = g_ref[:, GDN_DIM + h * hd:GDN_DIM + (h + 1) * hd]
        k = k16.astype(F32)
        v = g_ref[:, 2 * GDN_DIM + h * hd:2 * GDN_DIM + (h + 1) * hd].astype(F32)
        beta = beta_all[:, h:h + 1]
        gc = gc_all[:, GDN_HEADS + h:GDN_HEADS + h + 1]
        gc_row = gc_rows[GDN_HEADS + h:GDN_HEADS + h + 1, :]
        egc = egc_all[:, GDN_HEADS + h:GDN_HEADS + h + 1]
        decay = jnp.exp(jnp.where(causal, gc - gc_row, NEG_BIG))
        kb = k * beta
        a_mat = (_mm_nt(kb, k16) * decay).astype(BF16)
        qk = _mm_nt(q, k16) * decay
        qk_d.append([qk[c * GDN_CHUNK:(c + 1) * GDN_CHUNK, c * GDN_CHUNK:(c + 1) * GDN_CHUNK].astype(BF16)
                     for c in range(nchunk)])
        a_lv.append([a_mat * m for m in level_masks])
        qd.append((q * egc).astype(BF16))
        kd.append((k * ekd_all[:, GDN_HEADS + h:GDN_HEADS + h + 1]).astype(BF16))
        rhs.append(jnp.concatenate([v * beta, kb * egc], axis=1).astype(BF16))
        yield

    tinv = [eye - a_lv[h][0].astype(F32) for h in heads]
    for lv in range(1, GDN_LEVELS):
        for h in heads:
            t16 = tinv[h].astype(BF16)
            x = dot(a_lv[h][lv], t16).astype(BF16)
            tinv[h] = tinv[h] - dot(t16, x)
        yield
    for h in heads:
        sol = dot(tinv[h].astype(BF16), rhs[h])
        wq = [jnp.concatenate([sol[c * GDN_CHUNK:(c + 1) * GDN_CHUNK, hd:2 * hd].astype(BF16),
                               qd[h][c * GDN_CHUNK:(c + 1) * GDN_CHUNK]], axis=0) for c in range(nchunk)]
        staged.append((sol[:, 0:hd], wq, qk_d[h], kd[h]))
    staged.append(egc_all)
    yield


GDN_LOCAL_STEPS = 1 + GDN_HEADS + (GDN_LEVELS - 1) + 1


def _gdn_recurrence_steps(su_ref, swq_ref, sqk_ref, skd_ref, segc_ref, z_ref, nw_ref, o_ref, state_ref):
    tt = su_ref.shape[1]
    nchunk = tt // GDN_CHUNK
    hd = GDN_HEAD_DIM
    heads = range(GDN_HEADS)
    dot = functools.partial(jnp.dot, preferred_element_type=F32)
    states = [state_ref[h] for h in heads]
    for c in range(nchunk):
        r0, r1 = c * GDN_CHUNK, (c + 1) * GDN_CHUNK
        for h in heads:
            ws = dot(swq_ref[h, c], states[h].astype(BF16))
            v_new = (su_ref[h, r0:r1, :] - ws[0:GDN_CHUNK]).astype(BF16)
            o = ws[GDN_CHUNK:2 * GDN_CHUNK] + dot(sqk_ref[h, c], v_new)
            gl = segc_ref[r1 - 1:r1, GDN_HEADS + h:GDN_HEADS + h + 1]
            states[h] = states[h] * gl + lax.dot_general(skd_ref[h, r0:r1, :], v_new, (((0,), (0,)), ((), ())),
                                                         preferred_element_type=F32)
            o = o * lax.rsqrt(jnp.mean(o * o, axis=-1, keepdims=True) + NORM_EPS) * nw_ref[...]
            zz = z_ref[r0:r1, h * hd:(h + 1) * hd].astype(F32)
            o_ref[r0:r1, h * hd:(h + 1) * hd] = (o * _silu(zz)).astype(BF16)
        if c == nchunk - 1:
            for h in heads:
                state_ref[h] = states[h]
        yield


def _mixer_kernel(sink_ref, q_ref, kvc_ref, kvp_ref, bias_ref, g_ref, ba_ref, gp_ref, z_ref, nw_ref,
                  attn_ref, gdn_ref, kv_ref, state_ref, su_ref, swq_ref, sqk_ref, skd_ref, segc_ref, *, tiles_per_seq):
    i = pl.program_id(0)
    ntiles = pl.num_programs(0) - 1
    cur = jnp.minimum(i, ntiles - 1)
    prev = jnp.maximum(i - 1, 0)

    @pl.when(i == 0)
    def _():
        su_ref[...] = jnp.zeros_like(su_ref)
        swq_ref[...] = jnp.zeros_like(swq_ref)
        sqk_ref[...] = jnp.zeros_like(sqk_ref)
        skd_ref[...] = jnp.zeros_like(skd_ref)
        segc_ref[...] = jnp.zeros_like(segc_ref)

    @pl.when(prev % tiles_per_seq == 0)
    def _():
        state_ref[...] = jnp.zeros_like(state_ref)

    staged = []
    rec = _gdn_recurrence_steps(su_ref, swq_ref, sqk_ref, skd_ref, segc_ref, z_ref, nw_ref, gdn_ref, state_ref)
    loc = _gdn_chunk_local_steps(g_ref, ba_ref, gp_ref, staged)
    att = _attn_steps(sink_ref, q_ref, kvc_ref, kvp_ref, bias_ref, attn_ref, kv_ref, cur % tiles_per_seq == 0)

    def advance(gen, n):
        for _ in range(n):
            next(gen, None)

    nrec = su_ref.shape[1] // GDN_CHUNK
    nloc = GDN_LOCAL_STEPS
    natt = 1 + (q_ref.shape[0] // WINDOW) * ATTN_KV_HEADS * (ATTN_GROUP // 2)
    for c in range(nrec):
        advance(rec, 1)
        advance(loc, -(-nloc // nrec))
        advance(att, -(-natt // nrec))
    advance(loc, nloc)
    advance(att, natt)

    for h in range(GDN_HEADS):
        u, wq, qk_d, kd = staged[h]
        su_ref[h] = u
        skd_ref[h] = kd
        for c in range(nrec):
            swq_ref[h, c] = wq[c]
            sqk_ref[h, c] = qk_d[c]
    segc_ref[...] = staged[GDN_HEADS]


def _mixers(qkv, sinks, bias, g, z, ba, a_log, dt_bias, norm_w, batch, seq):
    m = g.shape[0]
    tt = min(MIX_TT, seq)
    nt = seq // tt
    ntiles = batch * nt
    sub = tt // WINDOW
    nchunk = tt // GDN_CHUNK
    kv_col = ATTN_Q_DIM // (2 * ATTN_KV_DIM)
    gate_params = jnp.zeros((SUBLANES, GATE_DIM), F32)
    gate_params = gate_params.at[0, GDN_HEADS:2 * GDN_HEADS].set(a_log.astype(F32))
    gate_params = gate_params.at[1, GDN_HEADS:2 * GDN_HEADS].set(dt_bias.astype(F32))
    cur = lambda i: (jnp.minimum(i, ntiles - 1), 0)
    prev = lambda i: (jnp.maximum(i - 1, 0), 0)
    fixed = lambda i: (0, 0)
    hd = GDN_HEAD_DIM
    return pl.pallas_call(
        functools.partial(_mixer_kernel, tiles_per_seq=nt),
        grid=(ntiles + 1,),
        in_specs=[pl.BlockSpec(memory_space=pltpu.SMEM),
                  pl.BlockSpec((tt, ATTN_Q_DIM), cur),
                  pl.BlockSpec((tt, 2 * ATTN_KV_DIM), lambda i: (jnp.minimum(i, ntiles - 1), kv_col)),
                  pl.BlockSpec((WINDOW, 2 * ATTN_KV_DIM),
                               lambda i: (jnp.maximum(jnp.minimum(i, ntiles - 1) * sub - 1, 0), kv_col)),
                  pl.BlockSpec((2, ATTN_HEADS, 2 * WINDOW, WINDOW), lambda i: (0, 0, 0, 0)),
                  pl.BlockSpec((tt, GDN_QKV_DIM), cur),
                  pl.BlockSpec((tt, GATE_DIM), cur),
                  pl.BlockSpec((SUBLANES, GATE_DIM), fixed),
                  pl.BlockSpec((tt, GDN_DIM), prev),
                  pl.BlockSpec((1, GDN_HEAD_DIM), fixed)],
        out_specs=[pl.BlockSpec((tt, ATTN_Q_DIM), cur),
                   pl.BlockSpec((tt, GDN_DIM), prev)],
        out_shape=[jax.ShapeDtypeStruct((m, ATTN_Q_DIM), BF16),
                   jax.ShapeDtypeStruct((m, GDN_DIM), BF16)],
        scratch_shapes=[pltpu.VMEM((tt + WINDOW, 2 * ATTN_KV_DIM), BF16),
                        pltpu.VMEM((GDN_HEADS, hd, hd), F32),
                        pltpu.VMEM((GDN_HEADS, tt, hd), F32),
                        pltpu.VMEM((GDN_HEADS, nchunk, 2 * GDN_CHUNK, hd), BF16),
                        pltpu.VMEM((GDN_HEADS, nchunk, GDN_CHUNK, GDN_CHUNK), BF16),
                        pltpu.VMEM((GDN_HEADS, tt, hd), BF16),
                        pltpu.VMEM((tt, GATE_DIM), F32)],
        compiler_params=pltpu.CompilerParams(dimension_semantics=("arbitrary",),
                                             vmem_limit_bytes=VMEM_LIMIT),
        name="mixers",
    )(sinks, qkv, qkv, qkv, bias, g, ba, gate_params, z, norm_w.reshape(1, GDN_HEAD_DIM).astype(F32))


def _mix_ffn_kernel(x_ref, attn_ref, gdn_ref, wout_ref, nw_ref, win_ref, cw_ref, cb_ref, wd_ref, fnw_ref,
                    o_ref, gbuf_ref, carry_ref, act_ref, *, final):
    tm = x_ref.shape[0]
    halo = SUBLANES

    @pl.when(pl.program_id(1) == 0)
    def _():
        carry_ref[...] = jnp.zeros_like(carry_ref)

    dot = functools.partial(jnp.dot, preferred_element_type=F32)
    x1 = (x_ref[...] + dot(attn_ref[...], wout_ref[0:ATTN_Q_DIM, :])
          + dot(gdn_ref[...], wout_ref[ATTN_Q_DIM:MIX_DIM, :]))
    h = _rmsnorm(x1, nw_ref[...]).astype(BF16)

    for c in range(D_FF // FFN_FC):
        c0, c1 = c * FFN_FC, (c + 1) * FFN_FC
        gate = dot(h, win_ref[:, c0:c1])
        up = dot(h, win_ref[:, D_FF + c0:D_FF + c1])
        for i in range(FFN_FC // LANES):
            gbuf_ref[i, 0:halo, :] = carry_ref[:, c0 + i * LANES:c0 + (i + 1) * LANES]
            gbuf_ref[i, halo:halo + tm, :] = gate[:, i * LANES:(i + 1) * LANES]
        carry_ref[:, c0:c1] = gate[tm - halo:tm, :]
        conv = _add_rows(_mul_rows(gate, cw_ref[2, :, c0:c1]), cb_ref[:, c0:c1])
        prev = []
        for sh in (1, 2):
            parts = []
            for i in range(FFN_FC // LANES):
                parts.append(gbuf_ref[i, halo - sh:halo - sh + tm, :])
            prev.append(jnp.concatenate(parts, axis=1))
        conv = conv + _mul_rows(prev[0], cw_ref[1, :, c0:c1])
        conv = conv + _mul_rows(prev[1], cw_ref[0, :, c0:c1])
        act_ref[:, c0:c1] = (_silu(conv) * up).astype(BF16)

    out = x1 + dot(act_ref[...], wd_ref[...])
    if final:
        out = _rmsnorm(out, fnw_ref[...])
    o_ref[...] = out


def _mix_ffn(x, attn, gdn, w_out, norm_w, w_ffn_in, conv_w, conv_b, w_down, final_norm, batch, seq, final):
    m = x.shape[0]
    tm = min(FFN_TM, seq)
    nt = seq // tm
    row = lambda b, t: (b * nt + t, 0)
    fixed = lambda b, t: (0, 0)
    resident = functools.partial(pl.BlockSpec, index_map=fixed, pipeline_mode=pl.Buffered(1))
    return pl.pallas_call(
        functools.partial(_mix_ffn_kernel, final=final),
        grid=(batch, nt),
        in_specs=[pl.BlockSpec((tm, D_MODEL), row),
                  pl.BlockSpec((tm, ATTN_Q_DIM), row),
                  pl.BlockSpec((tm, GDN_DIM), row),
                  resident((MIX_DIM, D_MODEL)),
                  pl.BlockSpec((SUBLANES, D_MODEL), fixed),
                  resident((D_MODEL, 2 * D_FF)),
                  pl.BlockSpec((FFN_CONV, SUBLANES, D_FF), lambda b, t: (0, 0, 0)),
                  pl.BlockSpec((SUBLANES, D_FF), fixed),
                  resident((D_FF, D_MODEL)),
                  pl.BlockSpec((SUBLANES, D_MODEL), fixed)],
        out_specs=pl.BlockSpec((tm, D_MODEL), row),
        out_shape=jax.ShapeDtypeStruct((m, D_MODEL), F32),
        scratch_shapes=[pltpu.VMEM((FFN_FC // LANES, tm + SUBLANES, LANES), F32),
                        pltpu.VMEM((SUBLANES, D_FF), F32),
                        pltpu.VMEM((tm, D_FF), BF16)],
        compiler_params=pltpu.CompilerParams(dimension_semantics=("parallel", "arbitrary"),
                                             vmem_limit_bytes=VMEM_LIMIT),
        name="mix_ffn",
    )(x, attn, gdn, w_out, _rows8(norm_w), w_ffn_in, _rows8(conv_w), _rows8(conv_b), w_down, _rows8(final_norm))


def kernel(x, attn_norm, w_in, attn_sinks, gdn_conv_w, gdn_a_log, gdn_dt_bias, gdn_norm, w_out, ffn_norm,
           w_ffn_in, ffn_conv_w, ffn_conv_b, w_down, final_norm):
    batch, seq, _ = x.shape
    assert seq % WINDOW == 0 and seq % GDN_CHUNK == 0
    xf = x.reshape(batch * seq, D_MODEL).astype(F32)
    bias = _attn_bias()
    depth = w_in.shape[0]
    for l in range(depth):
        w_pad = jnp.pad(w_in[l], ((0, 0), (0, IN_PAD - IN_DIM))).astype(BF16)
        qkv, g, z, ba = _inproj(xf, attn_norm[l].astype(F32), w_pad, gdn_conv_w[l].astype(F32), batch, seq)
        attn, gdn = _mixers(qkv, attn_sinks[l].astype(F32), bias, g, z, ba, gdn_a_log[l], gdn_dt_bias[l], gdn_norm[l],
                            batch, seq)
        xf = _mix_ffn(xf, attn, gdn, w_out[l].astype(BF16), ffn_norm[l].astype(F32), w_ffn_in[l].astype(BF16),
                      ffn_conv_w[l].astype(F32), ffn_conv_b[l].astype(F32), w_down[l].astype(BF16),
                      final_norm.astype(F32), batch, seq, final=(l == depth - 1))
    return xf.reshape(batch, seq, D_MODEL).astype(x.dtype)
```

```python
import functools

import numpy as np
import jax
import jax.numpy as jnp
from jax import lax
from jax.experimental import pallas as pl
from jax.experimental.pallas import tpu as pltpu

D_MODEL = 1024
ATTN_HEADS = 8
ATTN_KV_HEADS = 2
ATTN_GROUP = ATTN_HEADS // ATTN_KV_HEADS
ATTN_HEAD_DIM = 64
WINDOW = 128
GDN_HEADS = 4
GDN_HEAD_DIM = 128
GDN_CHUNK = 64
GDN_CONV = 4
D_FF = 2816
FFN_CONV = 3
NORM_EPS = 1e-6

ATTN_Q_DIM = ATTN_HEADS * ATTN_HEAD_DIM
ATTN_KV_DIM = ATTN_KV_HEADS * ATTN_HEAD_DIM
ATTN_QKV_DIM = ATTN_Q_DIM + 2 * ATTN_KV_DIM
GDN_DIM = GDN_HEADS * GDN_HEAD_DIM
GDN_QKV_DIM = 3 * GDN_DIM
MIX_DIM = ATTN_Q_DIM + GDN_DIM
IN_DIM = ATTN_QKV_DIM + GDN_QKV_DIM + GDN_DIM + 2 * GDN_HEADS
LANES = 128
SUBLANES = 8
GATE_DIM = LANES
IN_PAD = IN_DIM - 2 * GDN_HEADS + GATE_DIM

IN_TM = 512
MIX_TT = 256
FFN_TM = 512
FFN_FC = 256
VMEM_LIMIT = 56 * 1024 * 1024

NEG_BIG = -1e30
BF16 = jnp.bfloat16
F32 = jnp.float32


def _mm_nt(a, b):
    return lax.dot_general(a.astype(BF16), b.astype(BF16), (((1,), (1,)), ((), ())),
                           preferred_element_type=F32)


def _split3(a):
    a1 = a.astype(BF16)
    r1 = a - a1.astype(F32)
    a2 = r1.astype(BF16)
    a3 = (r1 - a2.astype(F32)).astype(BF16)
    return a1, a2, a3


def _mul_rows(x, w8):
    n, d = x.shape
    return (x.reshape(n // SUBLANES, SUBLANES, d) * w8[None]).reshape(n, d)


def _add_rows(x, w8):
    n, d = x.shape
    return (x.reshape(n // SUBLANES, SUBLANES, d) + w8[None]).reshape(n, d)


def _rows8(w):
    w = w.astype(F32)
    return jnp.broadcast_to(w[..., None, :], w.shape[:-1] + (SUBLANES, w.shape[-1]))


def _rmsnorm(x, w8):
    return _mul_rows(x * lax.rsqrt(jnp.mean(x * x, axis=-1, keepdims=True) + NORM_EPS), w8)


def _silu(x):
    return x / (1.0 + jnp.exp(-x))


def _inproj_kernel(x_ref, nw_ref, w_ref, cw_ref, qkv_ref, g_ref, z_ref, ba_ref, gbuf_ref):
    tm = x_ref.shape[0]
    hd = GDN_HEAD_DIM
    halo = SUBLANES
    o0, o1, o2, o3 = 0, ATTN_QKV_DIM, ATTN_QKV_DIM + GDN_QKV_DIM, IN_PAD - GATE_DIM
    dot = functools.partial(jnp.dot, preferred_element_type=F32)

    nblk = GDN_QKV_DIM // hd

    @pl.when(pl.program_id(1) == 0)
    def _():
        gbuf_ref[:, 0:halo, :] = jnp.zeros((nblk, halo, hd), F32)

    h = _rmsnorm(x_ref[...], nw_ref[...]).astype(BF16)
    graw = dot(h, w_ref[:, o1:o2])
    for blk in range(nblk):
        gbuf_ref[blk, halo:halo + tm, :] = graw[:, blk * hd:(blk + 1) * hd]
    qkv_ref[...] = dot(h, w_ref[:, o0:o1]).astype(BF16)
    z_ref[...] = dot(h, w_ref[:, o2:o3]).astype(BF16)
    ba_ref[...] = dot(h, w_ref[:, o3:IN_PAD])

    for blk in range(nblk):
        c0, c1 = blk * hd, (blk + 1) * hd
        acc = None
        for j in range(GDN_CONV):
            sh = GDN_CONV - 1 - j
            term = _mul_rows(gbuf_ref[blk, halo - sh:halo - sh + tm, :], cw_ref[j, :, c0:c1])
            acc = term if acc is None else acc + term
        y = _silu(acc)
        part = blk // GDN_HEADS
        if part < 2:
            y = y * lax.rsqrt(jnp.sum(y * y, axis=-1, keepdims=True) + NORM_EPS)
        if part == 0:
            y = y * (hd ** -0.5)
        g_ref[:, c0:c1] = y.astype(BF16)
    gbuf_ref[:, 0:halo, :] = gbuf_ref[:, tm:tm + halo, :]


def _inproj(x, norm_w8, w_pad, conv_w8, layer, batch, seq):
    m = x.shape[0]
    tm = min(IN_TM, seq)
    nt = seq // tm
    row = lambda b, t: (b * nt + t, 0)
    return pl.pallas_call(
        _inproj_kernel,
        grid=(batch, nt),
        in_specs=[pl.BlockSpec((tm, D_MODEL), row),
                  pl.BlockSpec((None, SUBLANES, D_MODEL), lambda b, t: (layer, 0, 0)),
                  pl.BlockSpec((None, D_MODEL, IN_PAD), lambda b, t: (layer, 0, 0), pipeline_mode=pl.Buffered(1)),
                  pl.BlockSpec((None, GDN_CONV, SUBLANES, GDN_QKV_DIM), lambda b, t: (layer, 0, 0, 0))],
        out_specs=[pl.BlockSpec((tm, ATTN_QKV_DIM), row),
                   pl.BlockSpec((tm, GDN_QKV_DIM), row),
                   pl.BlockSpec((tm, GDN_DIM), row),
                   pl.BlockSpec((tm, GATE_DIM), row)],
        out_shape=[jax.ShapeDtypeStruct((m, ATTN_QKV_DIM), BF16),
                   jax.ShapeDtypeStruct((m, GDN_QKV_DIM), BF16),
                   jax.ShapeDtypeStruct((m, GDN_DIM), BF16),
                   jax.ShapeDtypeStruct((m, GATE_DIM), F32)],
        scratch_shapes=[pltpu.VMEM((GDN_QKV_DIM // GDN_HEAD_DIM, tm + SUBLANES, GDN_HEAD_DIM), F32)],
        compiler_params=pltpu.CompilerParams(dimension_semantics=("parallel", "arbitrary"),
                                             vmem_limit_bytes=VMEM_LIMIT),
        name="inproj",
    )(x, norm_w8, w_pad, conv_w8)


def _attn_bias():
    qpos = np.arange(WINDOW)[None, :] + WINDOW
    kpos = np.arange(2 * WINDOW)[:, None]
    rel = qpos - kpos
    band = (rel >= 0) & (rel < WINDOW)
    slopes = 2.0 ** (-8.0 * np.arange(1, ATTN_HEADS + 1) / ATTN_HEADS)
    alibi = -slopes[:, None, None] * rel[None].astype(np.float64)
    rest = np.where(band[None], alibi, NEG_BIG)
    first = np.where((band & (kpos >= WINDOW))[None], alibi, NEG_BIG)
    return jnp.asarray(np.stack([first, rest]), dtype=F32)


def _attn_steps(sink_ref, layer, q_ref, kvc_ref, kvp_ref, bias_ref, o_ref, kv_ref, first_tile):
    dh = ATTN_HEAD_DIM
    w = WINDOW
    nsub = q_ref.shape[0] // w
    nkv = kv_ref.shape[0]
    kv_ref[0:w, :] = kvp_ref[...]
    kv_ref[w:nkv, :] = kvc_ref[...]
    ones = jnp.ones((nkv, dh), F32)
    vt_ext = []
    for h in range(ATTN_KV_HEADS):
        vo = ATTN_KV_DIM + h * dh
        v_ext = jnp.concatenate([kv_ref[:, vo:vo + dh].astype(F32), ones], axis=1)
        vt_ext.append(v_ext.T.astype(BF16))
    yield
    for j in range(nsub):
        r0 = j * w
        bias_idx = jnp.where(first_tile, 0, 1) if j == 0 else 1
        for h in range(ATTN_KV_HEADS):
            k = kv_ref[r0:r0 + 2 * w, h * dh:(h + 1) * dh]
            vt = vt_ext[h][:, r0:r0 + 2 * w]
            for gp in range(ATTN_GROUP // 2):
                hq0 = h * ATTN_GROUP + 2 * gp
                res = []
                for hq in (hq0, hq0 + 1):
                    q = q_ref[r0:r0 + w, hq * dh:(hq + 1) * dh] * (dh ** -0.5)
                    s = _mm_nt(k, q) + bias_ref[bias_idx, hq]
                    sink = sink_ref[layer, hq]
                    m = jnp.maximum(jnp.max(s, axis=0, keepdims=True), sink)
                    p = jnp.exp(s - m).astype(BF16)
                    acc = jnp.dot(vt, p, preferred_element_type=F32)
                    den = acc[dh:dh + 1, :] + jnp.exp(sink - m)
                    res.append(acc[0:dh, :] / den)
                o_ref[r0:r0 + w, hq0 * dh:(hq0 + 2) * dh] = jnp.concatenate(res, axis=0).T.astype(BF16)
                yield


GDN_LEVELS = GDN_CHUNK.bit_length() - 1


def _gdn_chunk_local_steps(g_ref, ba_ref, gp_ref, staged):
    tt = g_ref.shape[0]
    nchunk = tt // GDN_CHUNK
    hd = GDN_HEAD_DIM
    heads = range(GDN_HEADS)
    dot = functools.partial(jnp.dot, preferred_element_type=F32)

    ba = ba_ref[...]
    beta_all = 1.0 / (1.0 + jnp.exp(-ba))
    pre = ba + gp_ref[1:2, :]
    softplus = jnp.maximum(pre, 0.0) + jnp.log(1.0 + jnp.exp(-jnp.abs(pre)))
    g_all = -jnp.exp(gp_ref[0:1, :]) * softplus

    ri = lax.broadcasted_iota(jnp.int32, (tt, tt), 0)
    ci = lax.broadcasted_iota(jnp.int32, (tt, tt), 1)
    same_chunk = (ri // GDN_CHUNK) == (ci // GDN_CHUNK)
    causal = same_chunk & (ri >= ci)

    ltri = jnp.where(causal, 1.0, 0.0).astype(BF16)
    g1, g2, g3 = _split3(g_all)
    gc_all = (dot(ltri, g3) + dot(ltri, g2)) + dot(ltri, g1)
    gc_rows = gc_all.T
    gcl_all = jnp.concatenate(
        [jnp.broadcast_to(gc_all[(c + 1) * GDN_CHUNK - 1:(c + 1) * GDN_CHUNK, :], (GDN_CHUNK, GATE_DIM))
         for c in range(nchunk)], axis=0)
    egc_all = jnp.exp(gc_all)
    ekd_all = jnp.exp(gcl_all - gc_all)

    level_masks = []
    for lv in range(GDN_LEVELS):
        s = 1 << lv
        lower_left = ((ri // (2 * s)) == (ci // (2 * s))) & ((ri // s) > (ci // s))
        level_masks.append(jnp.where(lower_left, 1.0, 0.0).astype(BF16))
    eye = jnp.where(ri == ci, 1.0, 0.0)
    yield

    a_lv, qk_d, qd, kd, rhs = [], [], [], [], []
    for h in heads:
        q = g_ref[:, h * hd:(h + 1) * hd].astype(F32)
        k16 = g_ref[:, GDN_DIM + h * hd:GDN_DIM + (h + 1) * hd]
        k = k16.astype(F32)
        v = g_ref[:, 2 * GDN_DIM + h * hd:2 * GDN_DIM + (h + 1) * hd].astype(F32)
        beta = beta_all[:, h:h + 1]
        gc = gc_all[:, GDN_HEADS + h:GDN_HEADS + h + 1]
        gc_row = gc_rows[GDN_HEADS + h:GDN_HEADS + h + 1, :]
        egc = egc_all[:, GDN_HEADS + h:GDN_HEADS + h + 1]
        decay = jnp.exp(jnp.where(causal, gc - gc_row, NEG_BIG))
        kb = k * beta
        a_mat = (_mm_nt(kb, k16) * decay).astype(BF16)
        qk = _mm_nt(q, k16) * decay
        qk_d.append([qk[c * GDN_CHUNK:(c + 1) * GDN_CHUNK, c * GDN_CHUNK:(c + 1) * GDN_CHUNK].astype(BF16)
                     for c in range(nchunk)])
        a_lv.append([a_mat * m for m in level_masks])
        qd.append((q * egc).astype(BF16))
        kd.append((k * ekd_all[:, GDN_HEADS + h:GDN_HEADS + h + 1]).astype(BF16))
        rhs.append(jnp.concatenate([v * beta, kb * egc], axis=1).astype(BF16))
        yield

    tinv = [eye - a_lv[h][0].astype(F32) for h in heads]
    for lv in range(1, GDN_LEVELS):
        for h in heads:
            t16 = tinv[h].astype(BF16)
            x = dot(a_lv[h][lv], t16).astype(BF16)
            tinv[h] = tinv[h] - dot(t16, x)
        yield
    for h in heads:
        sol = dot(tinv[h].astype(BF16), rhs[h])
        wq = [jnp.concatenate([sol[c * GDN_CHUNK:(c + 1) * GDN_CHUNK, hd:2 * hd].astype(BF16),
                               qd[h][c * GDN_CHUNK:(c + 1) * GDN_CHUNK]], axis=0) for c in range(nchunk)]
        staged.append((sol[:, 0:hd], wq, qk_d[h], kd[h]))
    staged.append(egc_all)
    yield


GDN_LOCAL_STEPS = 1 + GDN_HEADS + (GDN_LEVELS - 1) + 1


def _gdn_recurrence_steps(su_ref, swq_ref, sqk_ref, skd_ref, segc_ref, z_ref, nw_ref, o_ref, state_ref):
    tt = su_ref.shape[1]
    nchunk = tt // GDN_CHUNK
    hd = GDN_HEAD_DIM
    heads = range(GDN_HEADS)
    dot = functools.partial(jnp.dot, preferred_element_type=F32)
    states = [state_ref[h] for h in heads]
    for c in range(nchunk):
        r0, r1 = c * GDN_CHUNK, (c + 1) * GDN_CHUNK
        for h in heads:
            ws = dot(swq_ref[h, c], states[h].astype(BF16))
            v_new = (su_ref[h, r0:r1, :] - ws[0:GDN_CHUNK]).astype(BF16)
            o = ws[GDN_CHUNK:2 * GDN_CHUNK] + dot(sqk_ref[h, c], v_new)
            gl = segc_ref[r1 - 1:r1, GDN_HEADS + h:GDN_HEADS + h + 1]
            states[h] = states[h] * gl + lax.dot_general(skd_ref[h, r0:r1, :], v_new, (((0,), (0,)), ((), ())),
                                                         preferred_element_type=F32)
            o = o * lax.rsqrt(jnp.mean(o * o, axis=-1, keepdims=True) + NORM_EPS) * nw_ref[...]
            zz = z_ref[r0:r1, h * hd:(h + 1) * hd].astype(F32)
            o_ref[r0:r1, h * hd:(h + 1) * hd] = (o * _silu(zz)).astype(BF16)
        if c == nchunk - 1:
            for h in heads:
                state_ref[h] = states[h]
        yield


def _mixer_kernel(sink_ref, q_ref, kvc_ref, kvp_ref, bias_ref, g_ref, ba_ref, gp_ref, z_ref, nw_ref,
                  attn_ref, gdn_ref, kv_ref, state_ref, su_ref, swq_ref, sqk_ref, skd_ref, segc_ref, *, tiles_per_seq, layer):
    i = pl.program_id(0)
    ntiles = pl.num_programs(0) - 1
    cur = jnp.minimum(i, ntiles - 1)
    prev = jnp.maximum(i - 1, 0)

    @pl.when(i == 0)
    def _():
        su_ref[...] = jnp.zeros_like(su_ref)
        swq_ref[...] = jnp.zeros_like(swq_ref)
        sqk_ref[...] = jnp.zeros_like(sqk_ref)
        skd_ref[...] = jnp.zeros_like(skd_ref)
        segc_ref[...] = jnp.zeros_like(segc_ref)

    @pl.when(prev % tiles_per_seq == 0)
    def _():
        state_ref[...] = jnp.zeros_like(state_ref)

    staged = []
    rec = _gdn_recurrence_steps(su_ref, swq_ref, sqk_ref, skd_ref, segc_ref, z_ref, nw_ref, gdn_ref, state_ref)
    loc = _gdn_chunk_local_steps(g_ref, ba_ref, gp_ref, staged)
    att = _attn_steps(sink_ref, layer, q_ref, kvc_ref, kvp_ref, bias_ref, attn_ref, kv_ref, cur % tiles_per_seq == 0)

    def advance(gen, n):
        for _ in range(n):
            next(gen, None)

    nrec = su_ref.shape[1] // GDN_CHUNK
    nloc = GDN_LOCAL_STEPS
    natt = 1 + (q_ref.shape[0] // WINDOW) * ATTN_KV_HEADS * (ATTN_GROUP // 2)
    for c in range(nrec):
        advance(rec, 1)
        advance(loc, -(-nloc // nrec))
        advance(att, -(-natt // nrec))
    advance(loc, nloc)
    advance(att, natt)

    for h in range(GDN_HEADS):
        u, wq, qk_d, kd = staged[h]
        su_ref[h] = u
        skd_ref[h] = kd
        for c in range(nrec):
            swq_ref[h, c] = wq[c]
            sqk_ref[h, c] = qk_d[c]
    segc_ref[...] = staged[GDN_HEADS]


def _gate_params(a_log, dt_bias):
    gp = jnp.zeros((a_log.shape[0], SUBLANES, GATE_DIM), F32)
    gp = gp.at[:, 0, GDN_HEADS:2 * GDN_HEADS].set(a_log.astype(F32))
    return gp.at[:, 1, GDN_HEADS:2 * GDN_HEADS].set(dt_bias.astype(F32))


def _mixers(qkv, sinks, bias, g, z, ba, gate_params, norm_w, layer, batch, seq):
    m = g.shape[0]
    tt = min(MIX_TT, seq)
    nt = seq // tt
    ntiles = batch * nt
    sub = tt // WINDOW
    nchunk = tt // GDN_CHUNK
    kv_col = ATTN_Q_DIM // (2 * ATTN_KV_DIM)
    cur = lambda i: (jnp.minimum(i, ntiles - 1), 0)
    prev = lambda i: (jnp.maximum(i - 1, 0), 0)
    hd = GDN_HEAD_DIM
    return pl.pallas_call(
        functools.partial(_mixer_kernel, tiles_per_seq=nt, layer=layer),
        grid=(ntiles + 1,),
        in_specs=[pl.BlockSpec(memory_space=pltpu.SMEM),
                  pl.BlockSpec((tt, ATTN_Q_DIM), cur),
                  pl.BlockSpec((tt, 2 * ATTN_KV_DIM), lambda i: (jnp.minimum(i, ntiles - 1), kv_col)),
                  pl.BlockSpec((WINDOW, 2 * ATTN_KV_DIM),
                               lambda i: (jnp.maximum(jnp.minimum(i, ntiles - 1) * sub - 1, 0), kv_col)),
                  pl.BlockSpec((2, ATTN_HEADS, 2 * WINDOW, WINDOW), lambda i: (0, 0, 0, 0)),
                  pl.BlockSpec((tt, GDN_QKV_DIM), cur),
                  pl.BlockSpec((tt, GATE_DIM), cur),
                  pl.BlockSpec((None, SUBLANES, GATE_DIM), lambda i: (layer, 0, 0)),
                  pl.BlockSpec((tt, GDN_DIM), prev),
                  pl.BlockSpec((None, 1, GDN_HEAD_DIM), lambda i: (layer, 0, 0))],
        out_specs=[pl.BlockSpec((tt, ATTN_Q_DIM), cur),
                   pl.BlockSpec((tt, GDN_DIM), prev)],
        out_shape=[jax.ShapeDtypeStruct((m, ATTN_Q_DIM), BF16),
                   jax.ShapeDtypeStruct((m, GDN_DIM), BF16)],
        scratch_shapes=[pltpu.VMEM((tt + WINDOW, 2 * ATTN_KV_DIM), BF16),
                        pltpu.VMEM((GDN_HEADS, hd, hd), F32),
                        pltpu.VMEM((GDN_HEADS, tt, hd), F32),
                        pltpu.VMEM((GDN_HEADS, nchunk, 2 * GDN_CHUNK, hd), BF16),
                        pltpu.VMEM((GDN_HEADS, nchunk, GDN_CHUNK, GDN_CHUNK), BF16),
                        pltpu.VMEM((GDN_HEADS, tt, hd), BF16),
                        pltpu.VMEM((tt, GATE_DIM), F32)],
        compiler_params=pltpu.CompilerParams(dimension_semantics=("arbitrary",),
                                             vmem_limit_bytes=VMEM_LIMIT),
        name="mixers",
    )(sinks, qkv, qkv, qkv, bias, g, ba, gate_params, z, norm_w)


def _mix_ffn_kernel(x_ref, attn_ref, gdn_ref, wout_ref, nw_ref, win_ref, cw_ref, cb_ref, wd_ref, fnw_ref,
                    o_ref, gbuf_ref, carry_ref, act_ref, *, final, plain):
    tm = x_ref.shape[0]
    halo = SUBLANES
    if plain:
        return _mix_ffn_plain(x_ref, attn_ref, gdn_ref, wout_ref, nw_ref, win_ref, cw_ref, cb_ref, wd_ref, fnw_ref,
                              o_ref, gbuf_ref, carry_ref, act_ref, final=final)

    @pl.when(pl.program_id(1) == 0)
    def _():
        carry_ref[...] = jnp.zeros_like(carry_ref)

    dot = functools.partial(jnp.dot, preferred_element_type=F32)
    x1 = (x_ref[...] + dot(attn_ref[...], wout_ref[0:ATTN_Q_DIM, :])
          + dot(gdn_ref[...], wout_ref[ATTN_Q_DIM:MIX_DIM, :]))
    h = _rmsnorm(x1, nw_ref[...]).astype(BF16)

    for c in range(D_FF // FFN_FC):
        c0, c1 = c * FFN_FC, (c + 1) * FFN_FC
        gate = dot(h, win_ref[:, c0:c1])
        up = dot(h, win_ref[:, D_FF + c0:D_FF + c1])
        for i in range(FFN_FC // LANES):
            gbuf_ref[i, 0:halo, :] = carry_ref[:, c0 + i * LANES:c0 + (i + 1) * LANES]
            gbuf_ref[i, halo:halo + tm, :] = gate[:, i * LANES:(i + 1) * LANES]
        carry_ref[:, c0:c1] = gate[tm - halo:tm, :]
        conv = _add_rows(_mul_rows(gate, cw_ref[2, :, c0:c1]), cb_ref[:, c0:c1])
        prev = []
        for sh in (1, 2):
            parts = []
            for i in range(FFN_FC // LANES):
                parts.append(gbuf_ref[i, halo - sh:halo - sh + tm, :])
            prev.append(jnp.concatenate(parts, axis=1))
        conv = conv + _mul_rows(prev[0], cw_ref[1, :, c0:c1])
        conv = conv + _mul_rows(prev[1], cw_ref[0, :, c0:c1])
        act_ref[:, c0:c1] = (_silu(conv) * up).astype(BF16)

    out = x1 + dot(act_ref[...], wd_ref[...])
    if final:
        out = _rmsnorm(out, fnw_ref[...])
    o_ref[...] = out


def _mix_ffn_plain(x_ref, attn_ref, gdn_ref, wout_ref, nw_ref, win_ref, cw_ref, cb_ref, wd_ref, fnw_ref,
                   o_ref, gbuf_ref, carry_ref, act_ref, *, final):
    tm = x_ref.shape[0]
    halo = SUBLANES

    @pl.when(pl.program_id(1) == 0)
    def _():
        carry_ref[...] = jnp.zeros_like(carry_ref)

    def rms(x, w):
        return x * lax.rsqrt(jnp.mean(x * x, axis=-1, keepdims=True) + NORM_EPS) * w

    dot = functools.partial(jnp.dot, preferred_element_type=F32)
    x1 = (x_ref[...] + dot(attn_ref[...], wout_ref[0:ATTN_Q_DIM, :])
          + dot(gdn_ref[...], wout_ref[ATTN_Q_DIM:MIX_DIM, :]))
    h = rms(x1, nw_ref[0:1, :]).astype(BF16)
    for c in range(D_FF // FFN_FC):
        c0, c1 = c * FFN_FC, (c + 1) * FFN_FC
        gate = dot(h, win_ref[:, c0:c1])
        up = dot(h, win_ref[:, D_FF + c0:D_FF + c1])
        gbuf_ref[0:halo, :] = carry_ref[:, c0:c1]
        gbuf_ref[halo:halo + tm, :] = gate
        carry_ref[:, c0:c1] = gate[tm - halo:tm, :]
        conv = gate * cw_ref[2, 0:1, c0:c1] + cb_ref[0:1, c0:c1]
        conv = conv + gbuf_ref[halo - 1:halo - 1 + tm, :] * cw_ref[1, 0:1, c0:c1]
        conv = conv + gbuf_ref[halo - 2:halo - 2 + tm, :] * cw_ref[0, 0:1, c0:c1]
        act_ref[:, c0:c1] = (_silu(conv) * up).astype(BF16)
    out = x1 + dot(act_ref[...], wd_ref[...])
    if final:
        out = rms(out, fnw_ref[0:1, :])
    o_ref[...] = out


def _mix_ffn(x, attn, gdn, w_out, norm_w8, w_ffn_in, conv_w8, conv_b8, w_down, final_norm8, layer, batch, seq, final):
    m = x.shape[0]
    tm = min(FFN_TM, seq)
    nt = seq // tm
    row = lambda b, t: (b * nt + t, 0)
    at_layer = lambda b, t: (layer, 0, 0)
    resident = lambda shape: pl.BlockSpec((None,) + shape, at_layer, pipeline_mode=pl.Buffered(1))
    return pl.pallas_call(
        functools.partial(_mix_ffn_kernel, final=final, plain=(layer == 1)),
        grid=(batch, nt),
        in_specs=[pl.BlockSpec((tm, D_MODEL), row),
                  pl.BlockSpec((tm, ATTN_Q_DIM), row),
                  pl.BlockSpec((tm, GDN_DIM), row),
                  resident((MIX_DIM, D_MODEL)),
                  pl.BlockSpec((None, SUBLANES, D_MODEL), at_layer),
                  resident((D_MODEL, 2 * D_FF)),
                  pl.BlockSpec((None, FFN_CONV, SUBLANES, D_FF), lambda b, t: (layer, 0, 0, 0)),
                  pl.BlockSpec((None, SUBLANES, D_FF), at_layer),
                  resident((D_FF, D_MODEL)),
                  pl.BlockSpec((SUBLANES, D_MODEL), lambda b, t: (0, 0))],
        out_specs=pl.BlockSpec((tm, D_MODEL), row),
        out_shape=jax.ShapeDtypeStruct((m, D_MODEL), F32),
        scratch_shapes=[pltpu.VMEM((tm + SUBLANES, FFN_FC), F32) if layer == 1 else
                        pltpu.VMEM((FFN_FC // LANES, tm + SUBLANES, LANES), F32),
                        pltpu.VMEM((SUBLANES, D_FF), F32),
                        pltpu.VMEM((tm, D_FF), BF16)],
        compiler_params=pltpu.CompilerParams(dimension_semantics=("parallel", "arbitrary"),
                                             vmem_limit_bytes=VMEM_LIMIT),
        name="mix_ffn",
    )(x, attn, gdn, w_out, norm_w8, w_ffn_in, conv_w8, conv_b8, w_down, final_norm8)


def kernel(x, attn_norm, w_in, attn_sinks, gdn_conv_w, gdn_a_log, gdn_dt_bias, gdn_norm, w_out, ffn_norm,
           w_ffn_in, ffn_conv_w, ffn_conv_b, w_down, final_norm):
    batch, seq, _ = x.shape
    assert seq % WINDOW == 0 and seq % GDN_CHUNK == 0
    xf = x.reshape(batch * seq, D_MODEL).astype(F32)
    bias = _attn_bias()
    depth = w_in.shape[0]
    w_in_b = jnp.pad(w_in, ((0, 0), (0, 0), (0, IN_PAD - IN_DIM))).astype(BF16)
    w_out_b, w_ffn_in_b, w_down_b = w_out.astype(BF16), w_ffn_in.astype(BF16), w_down.astype(BF16)
    attn_norm8, gdn_conv8 = _rows8(attn_norm), _rows8(gdn_conv_w)
    ffn_norm8, ffn_conv8, ffn_bias8, final_norm8 = _rows8(ffn_norm), _rows8(ffn_conv_w), _rows8(ffn_conv_b), _rows8(final_norm)
    gate_params = _gate_params(gdn_a_log, gdn_dt_bias)
    sinks = attn_sinks.astype(F32)
    gdn_norm3 = gdn_norm.astype(F32)[:, None, :]
    for l in range(depth):
        qkv, g, z, ba = _inproj(xf, attn_norm8, w_in_b, gdn_conv8, l, batch, seq)
        attn, gdn = _mixers(qkv, sinks, bias, g, z, ba, gate_params, gdn_norm3, l, batch, seq)
        xf = _mix_ffn(xf, attn, gdn, w_out_b, ffn_norm8, w_ffn_in_b, ffn_conv8, ffn_bias8, w_down_b, final_norm8,
                      l, batch, seq, final=(l == depth - 1))
    return xf.reshape(batch, seq, D_MODEL).astype(x.dtype)
```

```python
import functools

import numpy as np
import jax
import jax.numpy as jnp
from jax import lax
from jax.experimental import pallas as pl
from jax.experimental.pallas import tpu as pltpu

D_MODEL = 1024
ATTN_HEADS = 8
ATTN_KV_HEADS = 2
ATTN_GROUP = ATTN_HEADS // ATTN_KV_HEADS
ATTN_HEAD_DIM = 64
WINDOW = 128
GDN_HEADS = 4
GDN_HEAD_DIM = 128
GDN_CHUNK = 64
GDN_CONV = 4
D_FF = 2816
FFN_CONV = 3
NORM_EPS = 1e-6

ATTN_Q_DIM = ATTN_HEADS * ATTN_HEAD_DIM
ATTN_KV_DIM = ATTN_KV_HEADS * ATTN_HEAD_DIM
ATTN_QKV_DIM = ATTN_Q_DIM + 2 * ATTN_KV_DIM
GDN_DIM = GDN_HEADS * GDN_HEAD_DIM
GDN_QKV_DIM = 3 * GDN_DIM
MIX_DIM = ATTN_Q_DIM + GDN_DIM
IN_DIM = ATTN_QKV_DIM + GDN_QKV_DIM + GDN_DIM + 2 * GDN_HEADS
LANES = 128
SUBLANES = 8
GATE_DIM = LANES
IN_PAD = IN_DIM - 2 * GDN_HEADS + GATE_DIM

IN_TM = 512
MIX_TT = 256
FFN_TM = 512
FFN_FC = 256
VMEM_LIMIT = 56 * 1024 * 1024

NEG_BIG = -1e30
BF16 = jnp.bfloat16
F32 = jnp.float32


def _mm_nt(a, b):
    return lax.dot_general(a.astype(BF16), b.astype(BF16), (((1,), (1,)), ((), ())),
                           preferred_element_type=F32)


def _split3(a):
    a1 = a.astype(BF16)
    r1 = a - a1.astype(F32)
    a2 = r1.astype(BF16)
    a3 = (r1 - a2.astype(F32)).astype(BF16)
    return a1, a2, a3


def _mul_rows(x, w8):
    n, d = x.shape
    return (x.reshape(n // SUBLANES, SUBLANES, d) * w8[None]).reshape(n, d)


def _add_rows(x, w8):
    n, d = x.shape
    return (x.reshape(n // SUBLANES, SUBLANES, d) + w8[None]).reshape(n, d)


def _rows8(w):
    w = w.astype(F32)
    return jnp.broadcast_to(w[..., None, :], w.shape[:-1] + (SUBLANES, w.shape[-1]))


def _rmsnorm(x, w8):
    return _mul_rows(x * lax.rsqrt(jnp.mean(x * x, axis=-1, keepdims=True) + NORM_EPS), w8)


def _silu(x):
    return x / (1.0 + jnp.exp(-x))


def _inproj_kernel(x_ref, nw_ref, w_ref, cw_ref, qkv_ref, g_ref, z_ref, ba_ref, gbuf_ref, *, plain):
    tm = x_ref.shape[0]
    hd = GDN_HEAD_DIM
    halo = SUBLANES
    o0, o1, o2, o3 = 0, ATTN_QKV_DIM, ATTN_QKV_DIM + GDN_QKV_DIM, IN_PAD - GATE_DIM
    dot = functools.partial(jnp.dot, preferred_element_type=F32)

    nblk = GDN_QKV_DIM // hd

    @pl.when(pl.program_id(1) == 0)
    def _():
        gbuf_ref[:, 0:halo, :] = jnp.zeros((nblk, halo, hd), F32)

    if plain:
        xx = x_ref[...]
        h = (xx * lax.rsqrt(jnp.mean(xx * xx, axis=-1, keepdims=True) + NORM_EPS) * nw_ref[0:1, :]).astype(BF16)
    else:
        h = _rmsnorm(x_ref[...], nw_ref[...]).astype(BF16)
    graw = dot(h, w_ref[:, o1:o2])
    for blk in range(nblk):
        gbuf_ref[blk, halo:halo + tm, :] = graw[:, blk * hd:(blk + 1) * hd]
    qkv_ref[...] = dot(h, w_ref[:, o0:o1]).astype(BF16)
    z_ref[...] = dot(h, w_ref[:, o2:o3]).astype(BF16)
    ba_ref[...] = dot(h, w_ref[:, o3:IN_PAD])

    for blk in range(nblk):
        c0, c1 = blk * hd, (blk + 1) * hd
        acc = None
        for j in range(GDN_CONV):
            sh = GDN_CONV - 1 - j
            win = gbuf_ref[blk, halo - sh:halo - sh + tm, :]
            term = win * cw_ref[j, 0:1, c0:c1] if plain else _mul_rows(win, cw_ref[j, :, c0:c1])
            acc = term if acc is None else acc + term
        y = _silu(acc)
        part = blk // GDN_HEADS
        if part < 2:
            y = y * lax.rsqrt(jnp.sum(y * y, axis=-1, keepdims=True) + NORM_EPS)
        if part == 0:
            y = y * (hd ** -0.5)
        g_ref[:, c0:c1] = y.astype(BF16)
    gbuf_ref[:, 0:halo, :] = gbuf_ref[:, tm:tm + halo, :]


def _inproj(x, norm_w8, w_pad, conv_w8, layer, batch, seq):
    m = x.shape[0]
    tm = min(IN_TM, seq)
    nt = seq // tm
    row = lambda b, t: (b * nt + t, 0)
    return pl.pallas_call(
        functools.partial(_inproj_kernel, plain=(layer == 1)),
        grid=(batch, nt),
        in_specs=[pl.BlockSpec((tm, D_MODEL), row),
                  pl.BlockSpec((None, SUBLANES, D_MODEL), lambda b, t: (layer, 0, 0)),
                  pl.BlockSpec((None, D_MODEL, IN_PAD), lambda b, t: (layer, 0, 0), pipeline_mode=pl.Buffered(1)),
                  pl.BlockSpec((None, GDN_CONV, SUBLANES, GDN_QKV_DIM), lambda b, t: (layer, 0, 0, 0))],
        out_specs=[pl.BlockSpec((tm, ATTN_QKV_DIM), row),
                   pl.BlockSpec((tm, GDN_QKV_DIM), row),
                   pl.BlockSpec((tm, GDN_DIM), row),
                   pl.BlockSpec((tm, GATE_DIM), row)],
        out_shape=[jax.ShapeDtypeStruct((m, ATTN_QKV_DIM), BF16),
                   jax.ShapeDtypeStruct((m, GDN_QKV_DIM), BF16),
                   jax.ShapeDtypeStruct((m, GDN_DIM), BF16),
                   jax.ShapeDtypeStruct((m, GATE_DIM), F32)],
        scratch_shapes=[pltpu.VMEM((GDN_QKV_DIM // GDN_HEAD_DIM, tm + SUBLANES, GDN_HEAD_DIM), F32)],
        compiler_params=pltpu.CompilerParams(dimension_semantics=("parallel", "arbitrary"),
                                             vmem_limit_bytes=VMEM_LIMIT),
        name="inproj",
    )(x, norm_w8, w_pad, conv_w8)


def _attn_bias():
    qpos = np.arange(WINDOW)[None, :] + WINDOW
    kpos = np.arange(2 * WINDOW)[:, None]
    rel = qpos - kpos
    band = (rel >= 0) & (rel < WINDOW)
    slopes = 2.0 ** (-8.0 * np.arange(1, ATTN_HEADS + 1) / ATTN_HEADS)
    alibi = -slopes[:, None, None] * rel[None].astype(np.float64)
    rest = np.where(band[None], alibi, NEG_BIG)
    first = np.where((band & (kpos >= WINDOW))[None], alibi, NEG_BIG)
    return jnp.asarray(np.stack([first, rest]), dtype=F32)


def _attn_steps(sink_ref, layer, q_ref, kvc_ref, kvp_ref, bias_ref, o_ref, kv_ref, first_tile):
    dh = ATTN_HEAD_DIM
    w = WINDOW
    nsub = q_ref.shape[0] // w
    nkv = kv_ref.shape[0]
    kv_ref[0:w, :] = kvp_ref[...]
    kv_ref[w:nkv, :] = kvc_ref[...]
    ones = jnp.ones((nkv, dh), F32)
    vt_ext = []
    for h in range(ATTN_KV_HEADS):
        vo = ATTN_KV_DIM + h * dh
        v_ext = jnp.concatenate([kv_ref[:, vo:vo + dh].astype(F32), ones], axis=1)
        vt_ext.append(v_ext.T.astype(BF16))
    yield
    for j in range(nsub):
        r0 = j * w
        bias_idx = jnp.where(first_tile, 0, 1) if j == 0 else 1
        for h in range(ATTN_KV_HEADS):
            k = kv_ref[r0:r0 + 2 * w, h * dh:(h + 1) * dh]
            vt = vt_ext[h][:, r0:r0 + 2 * w]
            for gp in range(ATTN_GROUP // 2):
                hq0 = h * ATTN_GROUP + 2 * gp
                res = []
                for hq in (hq0, hq0 + 1):
                    q = q_ref[r0:r0 + w, hq * dh:(hq + 1) * dh] * (dh ** -0.5)
                    s = _mm_nt(k, q) + bias_ref[bias_idx, hq]
                    sink = sink_ref[layer, hq]
                    m = jnp.maximum(jnp.max(s, axis=0, keepdims=True), sink)
                    p = jnp.exp(s - m).astype(BF16)
                    acc = jnp.dot(vt, p, preferred_element_type=F32)
                    den = acc[dh:dh + 1, :] + jnp.exp(sink - m)
                    res.append(acc[0:dh, :] / den)
                o_ref[r0:r0 + w, hq0 * dh:(hq0 + 2) * dh] = jnp.concatenate(res, axis=0).T.astype(BF16)
                yield


GDN_LEVELS = GDN_CHUNK.bit_length() - 1


def _gdn_chunk_local_steps(g_ref, ba_ref, gp_ref, staged):
    tt = g_ref.shape[0]
    nchunk = tt // GDN_CHUNK
    hd = GDN_HEAD_DIM
    heads = range(GDN_HEADS)
    dot = functools.partial(jnp.dot, preferred_element_type=F32)

    ba = ba_ref[...]
    beta_all = 1.0 / (1.0 + jnp.exp(-ba))
    pre = ba + gp_ref[1:2, :]
    softplus = jnp.maximum(pre, 0.0) + jnp.log(1.0 + jnp.exp(-jnp.abs(pre)))
    g_all = -jnp.exp(gp_ref[0:1, :]) * softplus

    ri = lax.broadcasted_iota(jnp.int32, (tt, tt), 0)
    ci = lax.broadcasted_iota(jnp.int32, (tt, tt), 1)
    same_chunk = (ri // GDN_CHUNK) == (ci // GDN_CHUNK)
    causal = same_chunk & (ri >= ci)

    ltri = jnp.where(causal, 1.0, 0.0).astype(BF16)
    g1, g2, g3 = _split3(g_all)
    gc_all = (dot(ltri, g3) + dot(ltri, g2)) + dot(ltri, g1)
    gc_rows = gc_all.T
    gcl_all = jnp.concatenate(
        [jnp.broadcast_to(gc_all[(c + 1) * GDN_CHUNK - 1:(c + 1) * GDN_CHUNK, :], (GDN_CHUNK, GATE_DIM))
         for c in range(nchunk)], axis=0)
    egc_all = jnp.exp(gc_all)
    ekd_all = jnp.exp(gcl_all - gc_all)

    level_masks = []
    for lv in range(GDN_LEVELS):
        s = 1 << lv
        lower_left = ((ri // (2 * s)) == (ci // (2 * s))) & ((ri // s) > (ci // s))
        level_masks.append(jnp.where(lower_left, 1.0, 0.0).astype(BF16))
    eye = jnp.where(ri == ci, 1.0, 0.0)
    yield

    a_lv, qk_d, qd, kd, rhs = [], [], [], [], []
    for h in heads:
        q = g_ref[:, h * hd:(h + 1) * hd].astype(F32)
        k16 = g_ref[:, GDN_DIM + h * hd:GDN_DIM + (h + 1) * hd]
        k = k16.astype(F32)
        v = g_ref[:, 2 * GDN_DIM + h * hd:2 * GDN_DIM + (h + 1) * hd].astype(F32)
        beta = beta_all[:, h:h + 1]
        gc = gc_all[:, GDN_HEADS + h:GDN_HEADS + h + 1]
        gc_row = gc_rows[GDN_HEADS + h:GDN_HEADS + h + 1, :]
        egc = egc_all[:, GDN_HEADS + h:GDN_HEADS + h + 1]
        decay = jnp.exp(jnp.where(causal, gc - gc_row, NEG_BIG))
        kb = k * beta
        a_mat = (_mm_nt(kb, k16) * decay).astype(BF16)
        qk = _mm_nt(q, k16) * decay
        qk_d.append([qk[c * GDN_CHUNK:(c + 1) * GDN_CHUNK, c * GDN_CHUNK:(c + 1) * GDN_CHUNK].astype(BF16)
                     for c in range(nchunk)])
        a_lv.append([a_mat * m for m in level_masks])
        qd.append((q * egc).astype(BF16))
        kd.append((k * ekd_all[:, GDN_HEADS + h:GDN_HEADS + h + 1]).astype(BF16))
        rhs.append(jnp.concatenate([v * beta, kb * egc], axis=1).astype(BF16))
        yield

    tinv = [eye - a_lv[h][0].astype(F32) for h in heads]
    for lv in range(1, GDN_LEVELS):
        for h in heads:
            t16 = tinv[h].astype(BF16)
            x = dot(a_lv[h][lv], t16).astype(BF16)
            tinv[h] = tinv[h] - dot(t16, x)
        yield
    for h in heads:
        sol = dot(tinv[h].astype(BF16), rhs[h])
        wq = [jnp.concatenate([sol[c * GDN_CHUNK:(c + 1) * GDN_CHUNK, hd:2 * hd].astype(BF16),
                               qd[h][c * GDN_CHUNK:(c + 1) * GDN_CHUNK]], axis=0) for c in range(nchunk)]
        staged.append((sol[:, 0:hd], wq, qk_d[h], kd[h]))
    staged.append(egc_all)
    yield


GDN_LOCAL_STEPS = 1 + GDN_HEADS + (GDN_LEVELS - 1) + 1


def _gdn_recurrence_steps(su_ref, swq_ref, sqk_ref, skd_ref, segc_ref, z_ref, nw_ref, o_ref, state_ref):
    tt = su_ref.shape[1]
    nchunk = tt // GDN_CHUNK
    hd = GDN_HEAD_DIM
    heads = range(GDN_HEADS)
    dot = functools.partial(jnp.dot, preferred_element_type=F32)
    states = [state_ref[h] for h in heads]
    for c in range(nchunk):
        r0, r1 = c * GDN_CHUNK, (c + 1) * GDN_CHUNK
        for h in heads:
            ws = dot(swq_ref[h, c], states[h].astype(BF16))
            v_new = (su_ref[h, r0:r1, :] - ws[0:GDN_CHUNK]).astype(BF16)
            o = ws[GDN_CHUNK:2 * GDN_CHUNK] + dot(sqk_ref[h, c], v_new)
            gl = segc_ref[r1 - 1:r1, GDN_HEADS + h:GDN_HEADS + h + 1]
            states[h] = states[h] * gl + lax.dot_general(skd_ref[h, r0:r1, :], v_new, (((0,), (0,)), ((), ())),
                                                         preferred_element_type=F32)
            o = o * lax.rsqrt(jnp.mean(o * o, axis=-1, keepdims=True) + NORM_EPS) * nw_ref[...]
            zz = z_ref[r0:r1, h * hd:(h + 1) * hd].astype(F32)
            o_ref[r0:r1, h * hd:(h + 1) * hd] = (o * _silu(zz)).astype(BF16)
        if c == nchunk - 1:
            for h in heads:
                state_ref[h] = states[h]
        yield


def _mixer_kernel(sink_ref, q_ref, kvc_ref, kvp_ref, bias_ref, g_ref, ba_ref, gp_ref, z_ref, nw_ref,
                  attn_ref, gdn_ref, kv_ref, state_ref, su_ref, swq_ref, sqk_ref, skd_ref, segc_ref, *, tiles_per_seq, layer):
    i = pl.program_id(0)
    ntiles = pl.num_programs(0) - 1
    cur = jnp.minimum(i, ntiles - 1)
    prev = jnp.maximum(i - 1, 0)

    @pl.when(i == 0)
    def _():
        su_ref[...] = jnp.zeros_like(su_ref)
        swq_ref[...] = jnp.zeros_like(swq_ref)
        sqk_ref[...] = jnp.zeros_like(sqk_ref)
        skd_ref[...] = jnp.zeros_like(skd_ref)
        segc_ref[...] = jnp.zeros_like(segc_ref)

    @pl.when(prev % tiles_per_seq == 0)
    def _():
        state_ref[...] = jnp.zeros_like(state_ref)

    staged = []
    rec = _gdn_recurrence_steps(su_ref, swq_ref, sqk_ref, skd_ref, segc_ref, z_ref, nw_ref, gdn_ref, state_ref)
    loc = _gdn_chunk_local_steps(g_ref, ba_ref, gp_ref, staged)
    att = _attn_steps(sink_ref, layer, q_ref, kvc_ref, kvp_ref, bias_ref, attn_ref, kv_ref, cur % tiles_per_seq == 0)

    def advance(gen, n):
        for _ in range(n):
            next(gen, None)

    nrec = su_ref.shape[1] // GDN_CHUNK
    nloc = GDN_LOCAL_STEPS
    natt = 1 + (q_ref.shape[0] // WINDOW) * ATTN_KV_HEADS * (ATTN_GROUP // 2)
    for c in range(nrec):
        advance(rec, 1)
        advance(loc, -(-nloc // nrec))
        advance(att, -(-natt // nrec))
    advance(loc, nloc)
    advance(att, natt)

    for h in range(GDN_HEADS):
        u, wq, qk_d, kd = staged[h]
        su_ref[h] = u
        skd_ref[h] = kd
        for c in range(nrec):
            swq_ref[h, c] = wq[c]
            sqk_ref[h, c] = qk_d[c]
    segc_ref[...] = staged[GDN_HEADS]


def _gate_params(a_log, dt_bias):
    gp = jnp.zeros((a_log.shape[0], SUBLANES, GATE_DIM), F32)
    gp = gp.at[:, 0, GDN_HEADS:2 * GDN_HEADS].set(a_log.astype(F32))
    return gp.at[:, 1, GDN_HEADS:2 * GDN_HEADS].set(dt_bias.astype(F32))


def _mixers(qkv, sinks, bias, g, z, ba, gate_params, norm_w, layer, batch, seq):
    m = g.shape[0]
    tt = min(MIX_TT, seq)
    nt = seq // tt
    ntiles = batch * nt
    sub = tt // WINDOW
    nchunk = tt // GDN_CHUNK
    kv_col = ATTN_Q_DIM // (2 * ATTN_KV_DIM)
    cur = lambda i: (jnp.minimum(i, ntiles - 1), 0)
    prev = lambda i: (jnp.maximum(i - 1, 0), 0)
    hd = GDN_HEAD_DIM
    return pl.pallas_call(
        functools.partial(_mixer_kernel, tiles_per_seq=nt, layer=layer),
        grid=(ntiles + 1,),
        in_specs=[pl.BlockSpec(memory_space=pltpu.SMEM),
                  pl.BlockSpec((tt, ATTN_Q_DIM), cur),
                  pl.BlockSpec((tt, 2 * ATTN_KV_DIM), lambda i: (jnp.minimum(i, ntiles - 1), kv_col)),
                  pl.BlockSpec((WINDOW, 2 * ATTN_KV_DIM),
                               lambda i: (jnp.maximum(jnp.minimum(i, ntiles - 1) * sub - 1, 0), kv_col)),
                  pl.BlockSpec((2, ATTN_HEADS, 2 * WINDOW, WINDOW), lambda i: (0, 0, 0, 0)),
                  pl.BlockSpec((tt, GDN_QKV_DIM), cur),
                  pl.BlockSpec((tt, GATE_DIM), cur),
                  pl.BlockSpec((None, SUBLANES, GATE_DIM), lambda i: (layer, 0, 0)),
                  pl.BlockSpec((tt, GDN_DIM), prev),
                  pl.BlockSpec((None, 1, GDN_HEAD_DIM), lambda i: (layer, 0, 0))],
        out_specs=[pl.BlockSpec((tt, ATTN_Q_DIM), cur),
                   pl.BlockSpec((tt, GDN_DIM), prev)],
        out_shape=[jax.ShapeDtypeStruct((m, ATTN_Q_DIM), BF16),
                   jax.ShapeDtypeStruct((m, GDN_DIM), BF16)],
        scratch_shapes=[pltpu.VMEM((tt + WINDOW, 2 * ATTN_KV_DIM), BF16),
                        pltpu.VMEM((GDN_HEADS, hd, hd), F32),
                        pltpu.VMEM((GDN_HEADS, tt, hd), F32),
                        pltpu.VMEM((GDN_HEADS, nchunk, 2 * GDN_CHUNK, hd), BF16),
                        pltpu.VMEM((GDN_HEADS, nchunk, GDN_CHUNK, GDN_CHUNK), BF16),
                        pltpu.VMEM((GDN_HEADS, tt, hd), BF16),
                        pltpu.VMEM((tt, GATE_DIM), F32)],
        compiler_params=pltpu.CompilerParams(dimension_semantics=("arbitrary",),
                                             vmem_limit_bytes=VMEM_LIMIT),
        name="mixers",
    )(sinks, qkv, qkv, qkv, bias, g, ba, gate_params, z, norm_w)


def _mix_ffn_kernel(x_ref, attn_ref, gdn_ref, wout_ref, nw_ref, win_ref, cw_ref, cb_ref, wd_ref, fnw_ref,
                    o_ref, gbuf_ref, carry_ref, act_ref, *, final):
    tm = x_ref.shape[0]
    halo = SUBLANES

    @pl.when(pl.program_id(1) == 0)
    def _():
        carry_ref[...] = jnp.zeros_like(carry_ref)

    def rms(x, w):
        return x * lax.rsqrt(jnp.mean(x * x, axis=-1, keepdims=True) + NORM_EPS) * w

    dot = functools.partial(jnp.dot, preferred_element_type=F32)
    x1 = (x_ref[...] + dot(attn_ref[...], wout_ref[0:ATTN_Q_DIM, :])
          + dot(gdn_ref[...], wout_ref[ATTN_Q_DIM:MIX_DIM, :]))
    h = rms(x1, nw_ref[0:1, :]).astype(BF16)
    for c in range(D_FF // FFN_FC):
        c0, c1 = c * FFN_FC, (c + 1) * FFN_FC
        gate = dot(h, win_ref[:, c0:c1])
        up = dot(h, win_ref[:, D_FF + c0:D_FF + c1])
        gbuf_ref[0:halo, :] = carry_ref[:, c0:c1]
        gbuf_ref[halo:halo + tm, :] = gate
        carry_ref[:, c0:c1] = gate[tm - halo:tm, :]
        conv = gate * cw_ref[2, 0:1, c0:c1] + cb_ref[0:1, c0:c1]
        conv = conv + gbuf_ref[halo - 1:halo - 1 + tm, :] * cw_ref[1, 0:1, c0:c1]
        conv = conv + gbuf_ref[halo - 2:halo - 2 + tm, :] * cw_ref[0, 0:1, c0:c1]
        act_ref[:, c0:c1] = (_silu(conv) * up).astype(BF16)
    out = x1 + dot(act_ref[...], wd_ref[...])
    if final:
        out = rms(out, fnw_ref[0:1, :])
    o_ref[...] = out


def _mix_ffn(x, attn, gdn, w_out, norm_w8, w_ffn_in, conv_w8, conv_b8, w_down, final_norm8, layer, batch, seq, final):
    m = x.shape[0]
    tm = min(FFN_TM, seq)
    nt = seq // tm
    row = lambda b, t: (b * nt + t, 0)
    at_layer = lambda b, t: (layer, 0, 0)
    resident = lambda shape: pl.BlockSpec((None,) + shape, at_layer, pipeline_mode=pl.Buffered(1))
    return pl.pallas_call(
        functools.partial(_mix_ffn_kernel, final=final),
        grid=(batch, nt),
        in_specs=[pl.BlockSpec((tm, D_MODEL), row),
                  pl.BlockSpec((tm, ATTN_Q_DIM), row),
                  pl.BlockSpec((tm, GDN_DIM), row),
                  resident((MIX_DIM, D_MODEL)),
                  pl.BlockSpec((None, SUBLANES, D_MODEL), at_layer),
                  resident((D_MODEL, 2 * D_FF)),
                  pl.BlockSpec((None, FFN_CONV, SUBLANES, D_FF), lambda b, t: (layer, 0, 0, 0)),
                  pl.BlockSpec((None, SUBLANES, D_FF), at_layer),
                  resident((D_FF, D_MODEL)),
                  pl.BlockSpec((SUBLANES, D_MODEL), lambda b, t: (0, 0))],
        out_specs=pl.BlockSpec((tm, D_MODEL), row),
        out_shape=jax.ShapeDtypeStruct((m, D_MODEL), F32),
        scratch_shapes=[pltpu.VMEM((tm + SUBLANES, FFN_FC), F32),
                        pltpu.VMEM((SUBLANES, D_FF), F32),
                        pltpu.VMEM((tm, D_FF), BF16)],
        compiler_params=pltpu.CompilerParams(dimension_semantics=("parallel", "arbitrary"),
                                             vmem_limit_bytes=VMEM_LIMIT),
        name="mix_ffn",
    )(x, attn, gdn, w_out, norm_w8, w_ffn_in, conv_w8, conv_b8, w_down, final_norm8)


def kernel(x, attn_norm, w_in, attn_sinks, gdn_conv_w, gdn_a_log, gdn_dt_bias, gdn_norm, w_out, ffn_norm,
           w_ffn_in, ffn_conv_w, ffn_conv_b, w_down, final_norm):
    batch, seq, _ = x.shape
    assert seq % WINDOW == 0 and seq % GDN_CHUNK == 0
    xf = x.reshape(batch * seq, D_MODEL).astype(F32)
    bias = _attn_bias()
    depth = w_in.shape[0]
    w_in_b = jnp.pad(w_in, ((0, 0), (0, 0), (0, IN_PAD - IN_DIM))).astype(BF16)
    w_out_b, w_ffn_in_b, w_down_b = w_out.astype(BF16), w_ffn_in.astype(BF16), w_down.astype(BF16)
    attn_norm8, gdn_conv8 = _rows8(attn_norm), _rows8(gdn_conv_w)
    ffn_norm8, ffn_conv8, ffn_bias8, final_norm8 = _rows8(ffn_norm), _rows8(ffn_conv_w), _rows8(ffn_conv_b), _rows8(final_norm)
    gate_params = _gate_params(gdn_a_log, gdn_dt_bias)
    sinks = attn_sinks.astype(F32)
    gdn_norm3 = gdn_norm.astype(F32)[:, None, :]
    for l in range(depth):
        qkv, g, z, ba = _inproj(xf, attn_norm8, w_in_b, gdn_conv8, l, batch, seq)
        attn, gdn = _mixers(qkv, sinks, bias, g, z, ba, gate_params, gdn_norm3, l, batch, seq)
        xf = _mix_ffn(xf, attn, gdn, w_out_b, ffn_norm8, w_ffn_in_b, ffn_conv8, ffn_bias8, w_down_b, final_norm8,
                      l, batch, seq, final=(l == depth - 1))
    return xf.reshape(batch, seq, D_MODEL).astype(x.dtype)
```

```python
import functools

import numpy as np
import jax
import jax.numpy as jnp
from jax import lax
from jax.experimental import pallas as pl
from jax.experimental.pallas import tpu as pltpu

D_MODEL = 1024
ATTN_HEADS = 8
ATTN_KV_HEADS = 2
ATTN_GROUP = ATTN_HEADS // ATTN_KV_HEADS
ATTN_HEAD_DIM = 64
WINDOW = 128
GDN_HEADS = 4
GDN_HEAD_DIM = 128
GDN_CHUNK = 64
GDN_CONV = 4
D_FF = 2816
FFN_CONV = 3
NORM_EPS = 1e-6

ATTN_Q_DIM = ATTN_HEADS * ATTN_HEAD_DIM
ATTN_KV_DIM = ATTN_KV_HEADS * ATTN_HEAD_DIM
ATTN_QKV_DIM = ATTN_Q_DIM + 2 * ATTN_KV_DIM
GDN_DIM = GDN_HEADS * GDN_HEAD_DIM
GDN_QKV_DIM = 3 * GDN_DIM
MIX_DIM = ATTN_Q_DIM + GDN_DIM
IN_DIM = ATTN_QKV_DIM + GDN_QKV_DIM + GDN_DIM + 2 * GDN_HEADS
LANES = 128
SUBLANES = 8
GATE_DIM = LANES
IN_PAD = IN_DIM - 2 * GDN_HEADS + GATE_DIM

IN_TM = 512
MIX_TT = 256
FFN_TM = 512
FFN_FC = 256
VMEM_LIMIT = 56 * 1024 * 1024

NEG_BIG = -1e30
BF16 = jnp.bfloat16
F32 = jnp.float32


def _mm_nt(a, b):
    return lax.dot_general(a.astype(BF16), b.astype(BF16), (((1,), (1,)), ((), ())),
                           preferred_element_type=F32)


def _split3(a):
    a1 = a.astype(BF16)
    r1 = a - a1.astype(F32)
    a2 = r1.astype(BF16)
    a3 = (r1 - a2.astype(F32)).astype(BF16)
    return a1, a2, a3


def _mul_rows(x, w8):
    n, d = x.shape
    return (x.reshape(n // SUBLANES, SUBLANES, d) * w8[None]).reshape(n, d)


def _rows8(w):
    w = w.astype(F32)
    return jnp.broadcast_to(w[..., None, :], w.shape[:-1] + (SUBLANES, w.shape[-1]))


def _rmsnorm(x, w8):
    return _mul_rows(x * lax.rsqrt(jnp.mean(x * x, axis=-1, keepdims=True) + NORM_EPS), w8)


def _silu(x):
    return x / (1.0 + jnp.exp(-x))


def _inproj_kernel(x_ref, nw_ref, w_ref, cw_ref, qkv_ref, g_ref, z_ref, ba_ref, gbuf_ref):
    tm = x_ref.shape[0]
    hd = GDN_HEAD_DIM
    halo = SUBLANES
    o0, o1, o2, o3 = 0, ATTN_QKV_DIM, ATTN_QKV_DIM + GDN_QKV_DIM, IN_PAD - GATE_DIM
    dot = functools.partial(jnp.dot, preferred_element_type=F32)

    nblk = GDN_QKV_DIM // hd

    @pl.when(pl.program_id(1) == 0)
    def _():
        gbuf_ref[:, 0:halo, :] = jnp.zeros((nblk, halo, hd), F32)

    h = _rmsnorm(x_ref[...], nw_ref[...]).astype(BF16)
    graw = dot(h, w_ref[:, o1:o2])
    for blk in range(nblk):
        gbuf_ref[blk, halo:halo + tm, :] = graw[:, blk * hd:(blk + 1) * hd]
    qkv_ref[...] = dot(h, w_ref[:, o0:o1]).astype(BF16)
    z_ref[...] = dot(h, w_ref[:, o2:o3]).astype(BF16)
    ba_ref[...] = dot(h, w_ref[:, o3:IN_PAD])

    for blk in range(nblk):
        c0, c1 = blk * hd, (blk + 1) * hd
        acc = None
        for j in range(GDN_CONV):
            sh = GDN_CONV - 1 - j
            term = _mul_rows(gbuf_ref[blk, halo - sh:halo - sh + tm, :], cw_ref[j, :, c0:c1])
            acc = term if acc is None else acc + term
        y = _silu(acc)
        part = blk // GDN_HEADS
        if part < 2:
            y = y * lax.rsqrt(jnp.sum(y * y, axis=-1, keepdims=True) + NORM_EPS)
        if part == 0:
            y = y * (hd ** -0.5)
        g_ref[:, c0:c1] = y.astype(BF16)
    gbuf_ref[:, 0:halo, :] = gbuf_ref[:, tm:tm + halo, :]


def _inproj(x, norm_w8, w_pad, conv_w8, layer, batch, seq):
    m = x.shape[0]
    tm = min(IN_TM, seq)
    nt = seq // tm
    row = lambda b, t: (b * nt + t, 0)
    return pl.pallas_call(
        _inproj_kernel,
        grid=(batch, nt),
        in_specs=[pl.BlockSpec((tm, D_MODEL), row),
                  pl.BlockSpec((None, SUBLANES, D_MODEL), lambda b, t: (layer, 0, 0)),
                  pl.BlockSpec((None, D_MODEL, IN_PAD), lambda b, t: (layer, 0, 0), pipeline_mode=pl.Buffered(1)),
                  pl.BlockSpec((None, GDN_CONV, SUBLANES, GDN_QKV_DIM), lambda b, t: (layer, 0, 0, 0))],
        out_specs=[pl.BlockSpec((tm, ATTN_QKV_DIM), row),
                   pl.BlockSpec((tm, GDN_QKV_DIM), row),
                   pl.BlockSpec((tm, GDN_DIM), row),
                   pl.BlockSpec((tm, GATE_DIM), row)],
        out_shape=[jax.ShapeDtypeStruct((m, ATTN_QKV_DIM), BF16),
                   jax.ShapeDtypeStruct((m, GDN_QKV_DIM), BF16),
                   jax.ShapeDtypeStruct((m, GDN_DIM), BF16),
                   jax.ShapeDtypeStruct((m, GATE_DIM), F32)],
        scratch_shapes=[pltpu.VMEM((GDN_QKV_DIM // GDN_HEAD_DIM, tm + SUBLANES, GDN_HEAD_DIM), F32)],
        compiler_params=pltpu.CompilerParams(dimension_semantics=("parallel", "arbitrary"),
                                             vmem_limit_bytes=VMEM_LIMIT),
        name="inproj",
    )(x, norm_w8, w_pad, conv_w8)


def _attn_bias():
    qpos = np.arange(WINDOW)[None, :] + WINDOW
    kpos = np.arange(2 * WINDOW)[:, None]
    rel = qpos - kpos
    band = (rel >= 0) & (rel < WINDOW)
    slopes = 2.0 ** (-8.0 * np.arange(1, ATTN_HEADS + 1) / ATTN_HEADS)
    alibi = -slopes[:, None, None] * rel[None].astype(np.float64)
    rest = np.where(band[None], alibi, NEG_BIG)
    first = np.where((band & (kpos >= WINDOW))[None], alibi, NEG_BIG)
    return jnp.asarray(np.stack([first, rest]), dtype=F32)


def _attn_steps(sink_ref, layer, q_ref, kvc_ref, kvp_ref, bias_ref, o_ref, kv_ref, first_tile):
    dh = ATTN_HEAD_DIM
    w = WINDOW
    nsub = q_ref.shape[0] // w
    nkv = kv_ref.shape[0]
    kv_ref[0:w, :] = kvp_ref[...]
    kv_ref[w:nkv, :] = kvc_ref[...]
    ones = jnp.ones((nkv, dh), F32)
    vt_ext = []
    for h in range(ATTN_KV_HEADS):
        vo = ATTN_KV_DIM + h * dh
        v_ext = jnp.concatenate([kv_ref[:, vo:vo + dh].astype(F32), ones], axis=1)
        vt_ext.append(v_ext.T.astype(BF16))
    yield
    for j in range(nsub):
        r0 = j * w
        bias_idx = jnp.where(first_tile, 0, 1) if j == 0 else 1
        for h in range(ATTN_KV_HEADS):
            k = kv_ref[r0:r0 + 2 * w, h * dh:(h + 1) * dh]
            vt = vt_ext[h][:, r0:r0 + 2 * w]
            for gp in range(ATTN_GROUP // 2):
                hq0 = h * ATTN_GROUP + 2 * gp
                res = []
                for hq in (hq0, hq0 + 1):
                    q = q_ref[r0:r0 + w, hq * dh:(hq + 1) * dh] * (dh ** -0.5)
                    s = _mm_nt(k, q) + bias_ref[bias_idx, hq]
                    sink = sink_ref[layer, hq]
                    m = jnp.maximum(jnp.max(s, axis=0, keepdims=True), sink)
                    p = jnp.exp(s - m).astype(BF16)
                    acc = jnp.dot(vt, p, preferred_element_type=F32)
                    den = acc[dh:dh + 1, :] + jnp.exp(sink - m)
                    res.append(acc[0:dh, :] / den)
                o_ref[r0:r0 + w, hq0 * dh:(hq0 + 2) * dh] = jnp.concatenate(res, axis=0).T.astype(BF16)
                yield


GDN_LEVELS = GDN_CHUNK.bit_length() - 1


def _gdn_chunk_local_steps(g_ref, ba_ref, gp_ref, staged):
    tt = g_ref.shape[0]
    nchunk = tt // GDN_CHUNK
    hd = GDN_HEAD_DIM
    heads = range(GDN_HEADS)
    dot = functools.partial(jnp.dot, preferred_element_type=F32)

    ba = ba_ref[...]
    beta_all = 1.0 / (1.0 + jnp.exp(-ba))
    pre = ba + gp_ref[1:2, :]
    softplus = jnp.maximum(pre, 0.0) + jnp.log(1.0 + jnp.exp(-jnp.abs(pre)))
    g_all = -jnp.exp(gp_ref[0:1, :]) * softplus

    ri = lax.broadcasted_iota(jnp.int32, (tt, tt), 0)
    ci = lax.broadcasted_iota(jnp.int32, (tt, tt), 1)
    causal = ((ri // GDN_CHUNK) == (ci // GDN_CHUNK)) & (ri >= ci)

    ltri = jnp.where(causal, 1.0, 0.0).astype(BF16)
    g1, g2, g3 = _split3(g_all)
    gc_all = (dot(ltri, g3) + dot(ltri, g2)) + dot(ltri, g1)
    gc_rows = gc_all.T
    gcl_all = jnp.concatenate(
        [jnp.broadcast_to(gc_all[(c + 1) * GDN_CHUNK - 1:(c + 1) * GDN_CHUNK, :], (GDN_CHUNK, GATE_DIM))
         for c in range(nchunk)], axis=0)
    egc_all = jnp.exp(gc_all)
    ekd_all = jnp.exp(gcl_all - gc_all)

    rc = lax.broadcasted_iota(jnp.int32, (GDN_CHUNK, GDN_CHUNK), 0)
    cc = lax.broadcasted_iota(jnp.int32, (GDN_CHUNK, GDN_CHUNK), 1)
    level_masks = []
    for lv in range(GDN_LEVELS):
        s = 1 << lv
        lower_left = ((rc // (2 * s)) == (cc // (2 * s))) & ((rc // s) > (cc // s))
        level_masks.append(jnp.where(lower_left, 1.0, 0.0).astype(BF16))
    eye = jnp.where(rc == cc, 1.0, 0.0)
    band = jnp.where(rc >= cc, 0.0, NEG_BIG)
    yield

    chunks = [slice(c * GDN_CHUNK, (c + 1) * GDN_CHUNK) for c in range(nchunk)]
    a_lv, qk_d, qd, kd, rhs = [], [], [], [], []
    for h in heads:
        q16 = g_ref[:, h * hd:(h + 1) * hd]
        k16 = g_ref[:, GDN_DIM + h * hd:GDN_DIM + (h + 1) * hd]
        q, k = q16.astype(F32), k16.astype(F32)
        v = g_ref[:, 2 * GDN_DIM + h * hd:2 * GDN_DIM + (h + 1) * hd].astype(F32)
        beta = beta_all[:, h:h + 1]
        gc = gc_all[:, GDN_HEADS + h:GDN_HEADS + h + 1]
        gc_row = gc_rows[GDN_HEADS + h:GDN_HEADS + h + 1, :]
        egc = egc_all[:, GDN_HEADS + h:GDN_HEADS + h + 1]
        kb = k * beta
        kb16 = kb.astype(BF16)
        a_h, qk_h = [], []
        for rs in chunks:
            decay = jnp.exp((gc[rs] - gc_row[:, rs]) + band)
            a_mat = (_mm_nt(kb16[rs], k16[rs]) * decay).astype(BF16)
            qk_h.append((_mm_nt(q16[rs], k16[rs]) * decay).astype(BF16))
            a_h.append([a_mat * m for m in level_masks])
        a_lv.append(a_h)
        qk_d.append(qk_h)
        qd.append((q * egc).astype(BF16))
        kd.append((k * ekd_all[:, GDN_HEADS + h:GDN_HEADS + h + 1]).astype(BF16))
        rhs.append(jnp.concatenate([v * beta, kb * egc], axis=1).astype(BF16))
        yield

    tinv = [[eye - a_lv[h][c][0].astype(F32) for c in range(nchunk)] for h in heads]
    for lv in range(1, GDN_LEVELS):
        for h in heads:
            for c in range(nchunk):
                t16 = tinv[h][c].astype(BF16)
                x = dot(a_lv[h][c][lv], t16).astype(BF16)
                tinv[h][c] = tinv[h][c] - dot(t16, x)
        yield
    for h in heads:
        sols = [dot(tinv[h][c].astype(BF16), rhs[h][chunks[c]]) for c in range(nchunk)]
        wq = [jnp.concatenate([sols[c][:, hd:2 * hd].astype(BF16), qd[h][chunks[c]]], axis=0) for c in range(nchunk)]
        staged.append((jnp.concatenate([sol[:, 0:hd] for sol in sols], axis=0), wq, qk_d[h], kd[h]))
    staged.append(egc_all)
    yield


GDN_LOCAL_STEPS = 1 + GDN_HEADS + (GDN_LEVELS - 1) + 1


def _gdn_recurrence_steps(su_ref, swq_ref, sqk_ref, skd_ref, segc_ref, z_ref, nw_ref, o_ref, state_ref):
    tt = su_ref.shape[1]
    nchunk = tt // GDN_CHUNK
    hd = GDN_HEAD_DIM
    heads = range(GDN_HEADS)
    dot = functools.partial(jnp.dot, preferred_element_type=F32)
    states = [state_ref[h] for h in heads]
    for c in range(nchunk):
        r0, r1 = c * GDN_CHUNK, (c + 1) * GDN_CHUNK
        for h in heads:
            ws = dot(swq_ref[h, c], states[h].astype(BF16))
            v_new = (su_ref[h, r0:r1, :] - ws[0:GDN_CHUNK]).astype(BF16)
            o = ws[GDN_CHUNK:2 * GDN_CHUNK] + dot(sqk_ref[h, c], v_new)
            gl = segc_ref[r1 - 1:r1, GDN_HEADS + h:GDN_HEADS + h + 1]
            states[h] = states[h] * gl + lax.dot_general(skd_ref[h, r0:r1, :], v_new, (((0,), (0,)), ((), ())),
                                                         preferred_element_type=F32)
            o = o * lax.rsqrt(jnp.mean(o * o, axis=-1, keepdims=True) + NORM_EPS) * nw_ref[...]
            zz = z_ref[r0:r1, h * hd:(h + 1) * hd].astype(F32)
            o_ref[r0:r1, h * hd:(h + 1) * hd] = (o * _silu(zz)).astype(BF16)
        if c == nchunk - 1:
            for h in heads:
                state_ref[h] = states[h]
        yield


def _mixer_kernel(sink_ref, q_ref, kvc_ref, kvp_ref, bias_ref, g_ref, ba_ref, gp_ref, z_ref, nw_ref,
                  attn_ref, gdn_ref, kv_ref, state_ref, su_ref, swq_ref, sqk_ref, skd_ref, segc_ref, *, tiles_per_seq, layer):
    i = pl.program_id(0)
    ntiles = pl.num_programs(0) - 1
    cur = jnp.minimum(i, ntiles - 1)
    prev = jnp.maximum(i - 1, 0)

    @pl.when(i == 0)
    def _():
        su_ref[...] = jnp.zeros_like(su_ref)
        swq_ref[...] = jnp.zeros_like(swq_ref)
        sqk_ref[...] = jnp.zeros_like(sqk_ref)
        skd_ref[...] = jnp.zeros_like(skd_ref)
        segc_ref[...] = jnp.zeros_like(segc_ref)

    @pl.when(prev % tiles_per_seq == 0)
    def _():
        state_ref[...] = jnp.zeros_like(state_ref)

    staged = []
    rec = _gdn_recurrence_steps(su_ref, swq_ref, sqk_ref, skd_ref, segc_ref, z_ref, nw_ref, gdn_ref, state_ref)
    loc = _gdn_chunk_local_steps(g_ref, ba_ref, gp_ref, staged)
    att = _attn_steps(sink_ref, layer, q_ref, kvc_ref, kvp_ref, bias_ref, attn_ref, kv_ref, cur % tiles_per_seq == 0)

    def advance(gen, n):
        for _ in range(n):
            next(gen, None)

    nrec = su_ref.shape[1] // GDN_CHUNK
    nloc = GDN_LOCAL_STEPS
    natt = 1 + (q_ref.shape[0] // WINDOW) * ATTN_KV_HEADS * (ATTN_GROUP // 2)
    for c in range(nrec):
        advance(rec, 1)
        advance(loc, -(-nloc // nrec))
        advance(att, -(-natt // nrec))
    advance(loc, nloc)
    advance(att, natt)

    for h in range(GDN_HEADS):
        u, wq, qk_d, kd = staged[h]
        su_ref[h] = u
        skd_ref[h] = kd
        for c in range(nrec):
            swq_ref[h, c] = wq[c]
            sqk_ref[h, c] = qk_d[c]
    segc_ref[...] = staged[GDN_HEADS]


def _gate_params(a_log, dt_bias):
    gp = jnp.zeros((a_log.shape[0], SUBLANES, GATE_DIM), F32)
    gp = gp.at[:, 0, GDN_HEADS:2 * GDN_HEADS].set(a_log.astype(F32))
    return gp.at[:, 1, GDN_HEADS:2 * GDN_HEADS].set(dt_bias.astype(F32))


def _mixers(qkv, sinks, bias, g, z, ba, gate_params, norm_w, layer, batch, seq):
    m = g.shape[0]
    tt = min(MIX_TT, seq)
    nt = seq // tt
    ntiles = batch * nt
    sub = tt // WINDOW
    nchunk = tt // GDN_CHUNK
    kv_col = ATTN_Q_DIM // (2 * ATTN_KV_DIM)
    cur = lambda i: (jnp.minimum(i, ntiles - 1), 0)
    prev = lambda i: (jnp.maximum(i - 1, 0), 0)
    hd = GDN_HEAD_DIM
    return pl.pallas_call(
        functools.partial(_mixer_kernel, tiles_per_seq=nt, layer=layer),
        grid=(ntiles + 1,),
        in_specs=[pl.BlockSpec(memory_space=pltpu.SMEM),
                  pl.BlockSpec((tt, ATTN_Q_DIM), cur),
                  pl.BlockSpec((tt, 2 * ATTN_KV_DIM), lambda i: (jnp.minimum(i, ntiles - 1), kv_col)),
                  pl.BlockSpec((WINDOW, 2 * ATTN_KV_DIM),
                               lambda i: (jnp.maximum(jnp.minimum(i, ntiles - 1) * sub - 1, 0), kv_col)),
                  pl.BlockSpec((2, ATTN_HEADS, 2 * WINDOW, WINDOW), lambda i: (0, 0, 0, 0)),
                  pl.BlockSpec((tt, GDN_QKV_DIM), cur),
                  pl.BlockSpec((tt, GATE_DIM), cur),
                  pl.BlockSpec((None, SUBLANES, GATE_DIM), lambda i: (layer, 0, 0)),
                  pl.BlockSpec((tt, GDN_DIM), prev),
                  pl.BlockSpec((None, 1, GDN_HEAD_DIM), lambda i: (layer, 0, 0))],
        out_specs=[pl.BlockSpec((tt, ATTN_Q_DIM), cur),
                   pl.BlockSpec((tt, GDN_DIM), prev)],
        out_shape=[jax.ShapeDtypeStruct((m, ATTN_Q_DIM), BF16),
                   jax.ShapeDtypeStruct((m, GDN_DIM), BF16)],
        scratch_shapes=[pltpu.VMEM((tt + WINDOW, 2 * ATTN_KV_DIM), BF16),
                        pltpu.VMEM((GDN_HEADS, hd, hd), F32),
                        pltpu.VMEM((GDN_HEADS, tt, hd), F32),
                        pltpu.VMEM((GDN_HEADS, nchunk, 2 * GDN_CHUNK, hd), BF16),
                        pltpu.VMEM((GDN_HEADS, nchunk, GDN_CHUNK, GDN_CHUNK), BF16),
                        pltpu.VMEM((GDN_HEADS, tt, hd), BF16),
                        pltpu.VMEM((tt, GATE_DIM), F32)],
        compiler_params=pltpu.CompilerParams(dimension_semantics=("arbitrary",),
                                             vmem_limit_bytes=VMEM_LIMIT),
        name="mixers",
    )(sinks, qkv, qkv, qkv, bias, g, ba, gate_params, z, norm_w)


def _mix_ffn_kernel(x_ref, attn_ref, gdn_ref, wout_ref, nw_ref, win_ref, cw_ref, cb_ref, wd_ref, fnw_ref,
                    o_ref, gbuf_ref, carry_ref, act_ref, *, final):
    tm = x_ref.shape[0]
    halo = SUBLANES

    @pl.when(pl.program_id(1) == 0)
    def _():
        carry_ref[...] = jnp.zeros_like(carry_ref)

    def rms(x, w):
        return x * lax.rsqrt(jnp.mean(x * x, axis=-1, keepdims=True) + NORM_EPS) * w

    dot = functools.partial(jnp.dot, preferred_element_type=F32)
    x1 = (x_ref[...] + dot(attn_ref[...], wout_ref[0:ATTN_Q_DIM, :])
          + dot(gdn_ref[...], wout_ref[ATTN_Q_DIM:MIX_DIM, :]))
    h = rms(x1, nw_ref[0:1, :]).astype(BF16)
    for c in range(D_FF // FFN_FC):
        c0, c1 = c * FFN_FC, (c + 1) * FFN_FC
        gate = dot(h, win_ref[:, c0:c1])
        up = dot(h, win_ref[:, D_FF + c0:D_FF + c1])
        gbuf_ref[0:halo, :] = carry_ref[:, c0:c1]
        gbuf_ref[halo:halo + tm, :] = gate
        carry_ref[:, c0:c1] = gate[tm - halo:tm, :]
        conv = gate * cw_ref[2, 0:1, c0:c1] + cb_ref[0:1, c0:c1]
        conv = conv + gbuf_ref[halo - 1:halo - 1 + tm, :] * cw_ref[1, 0:1, c0:c1]
        conv = conv + gbuf_ref[halo - 2:halo - 2 + tm, :] * cw_ref[0, 0:1, c0:c1]
        act_ref[:, c0:c1] = (_silu(conv) * up).astype(BF16)
    out = x1 + dot(act_ref[...], wd_ref[...])
    if final:
        out = rms(out, fnw_ref[0:1, :])
    o_ref[...] = out


def _mix_ffn(x, attn, gdn, w_out, norm_w8, w_ffn_in, conv_w8, conv_b8, w_down, final_norm8, layer, batch, seq, final):
    m = x.shape[0]
    tm = min(FFN_TM, seq)
    nt = seq // tm
    row = lambda b, t: (b * nt + t, 0)
    at_layer = lambda b, t: (layer, 0, 0)
    resident = lambda shape: pl.BlockSpec((None,) + shape, at_layer, pipeline_mode=pl.Buffered(1))
    return pl.pallas_call(
        functools.partial(_mix_ffn_kernel, final=final),
        grid=(batch, nt),
        in_specs=[pl.BlockSpec((tm, D_MODEL), row),
                  pl.BlockSpec((tm, ATTN_Q_DIM), row),
                  pl.BlockSpec((tm, GDN_DIM), row),
                  resident((MIX_DIM, D_MODEL)),
                  pl.BlockSpec((None, SUBLANES, D_MODEL), at_layer),
                  resident((D_MODEL, 2 * D_FF)),
                  pl.BlockSpec((None, FFN_CONV, SUBLANES, D_FF), lambda b, t: (layer, 0, 0, 0)),
                  pl.BlockSpec((None, SUBLANES, D_FF), at_layer),
                  resident((D_FF, D_MODEL)),
                  pl.BlockSpec((SUBLANES, D_MODEL), lambda b, t: (0, 0))],
        out_specs=pl.BlockSpec((tm, D_MODEL), row),
        out_shape=jax.ShapeDtypeStruct((m, D_MODEL), F32),
        scratch_shapes=[pltpu.VMEM((tm + SUBLANES, FFN_FC), F32),
                        pltpu.VMEM((SUBLANES, D_FF), F32),
                        pltpu.VMEM((tm, D_FF), BF16)],
        compiler_params=pltpu.CompilerParams(dimension_semantics=("parallel", "arbitrary"),
                                             vmem_limit_bytes=VMEM_LIMIT),
        name="mix_ffn",
    )(x, attn, gdn, w_out, norm_w8, w_ffn_in, conv_w8, conv_b8, w_down, final_norm8)


def kernel(x, attn_norm, w_in, attn_sinks, gdn_conv_w, gdn_a_log, gdn_dt_bias, gdn_norm, w_out, ffn_norm,
           w_ffn_in, ffn_conv_w, ffn_conv_b, w_down, final_norm):
    batch, seq, _ = x.shape
    assert seq % WINDOW == 0 and seq % GDN_CHUNK == 0
    xf = x.reshape(batch * seq, D_MODEL).astype(F32)
    bias = _attn_bias()
    depth = w_in.shape[0]
    w_in_b = jnp.pad(w_in, ((0, 0), (0, 0), (0, IN_PAD - IN_DIM))).astype(BF16)
    w_out_b, w_ffn_in_b, w_down_b = w_out.astype(BF16), w_ffn_in.astype(BF16), w_down.astype(BF16)
    attn_norm8, gdn_conv8 = _rows8(attn_norm), _rows8(gdn_conv_w)
    ffn_norm8, ffn_conv8, ffn_bias8, final_norm8 = _rows8(ffn_norm), _rows8(ffn_conv_w), _rows8(ffn_conv_b), _rows8(final_norm)
    gate_params = _gate_params(gdn_a_log, gdn_dt_bias)
    sinks = attn_sinks.astype(F32)
    gdn_norm3 = gdn_norm.astype(F32)[:, None, :]
    for l in range(depth):
        qkv, g, z, ba = _inproj(xf, attn_norm8, w_in_b, gdn_conv8, l, batch, seq)
        attn, gdn = _mixers(qkv, sinks, bias, g, z, ba, gate_params, gdn_norm3, l, batch, seq)
        xf = _mix_ffn(xf, attn, gdn, w_out_b, ffn_norm8, w_ffn_in_b, ffn_conv8, ffn_bias8, w_down_b, final_norm8,
                      l, batch, seq, final=(l == depth - 1))
    return xf.reshape(batch, seq, D_MODEL).astype(x.dtype)
```

```python
import functools

import numpy as np
import jax
import jax.numpy as jnp
from jax import lax
from jax.experimental import pallas as pl
from jax.experimental.pallas import tpu as pltpu

D_MODEL = 1024
ATTN_HEADS = 8
ATTN_KV_HEADS = 2
ATTN_GROUP = ATTN_HEADS // ATTN_KV_HEADS
ATTN_HEAD_DIM = 64
WINDOW = 128
GDN_HEADS = 4
GDN_HEAD_DIM = 128
GDN_CHUNK = 64
GDN_CONV = 4
D_FF = 2816
FFN_CONV = 3
NORM_EPS = 1e-6

ATTN_Q_DIM = ATTN_HEADS * ATTN_HEAD_DIM
ATTN_KV_DIM = ATTN_KV_HEADS * ATTN_HEAD_DIM
ATTN_QKV_DIM = ATTN_Q_DIM + 2 * ATTN_KV_DIM
GDN_DIM = GDN_HEADS * GDN_HEAD_DIM
GDN_QKV_DIM = 3 * GDN_DIM
MIX_DIM = ATTN_Q_DIM + GDN_DIM
IN_DIM = ATTN_QKV_DIM + GDN_QKV_DIM + GDN_DIM + 2 * GDN_HEADS
LANES = 128
SUBLANES = 8
GATE_DIM = LANES
IN_MAIN = IN_DIM - 2 * GDN_HEADS

IN_TM = 1024
MIX_TT = 256
FFN_TM = 512
FFN_FC = 256
VMEM_LIMIT = 56 * 1024 * 1024

NEG_BIG = -1e30
BF16 = jnp.bfloat16
F32 = jnp.float32


def _mm_nt(a, b):
    return lax.dot_general(a.astype(BF16), b.astype(BF16), (((1,), (1,)), ((), ())),
                           preferred_element_type=F32)


def _split3(a):
    a1 = a.astype(BF16)
    r1 = a - a1.astype(F32)
    a2 = r1.astype(BF16)
    a3 = (r1 - a2.astype(F32)).astype(BF16)
    return a1, a2, a3


def _mul_rows(x, w8):
    n, d = x.shape
    return (x.reshape(n // SUBLANES, SUBLANES, d) * w8[None]).reshape(n, d)


def _rows8(w):
    w = w.astype(F32)
    return jnp.broadcast_to(w[..., None, :], w.shape[:-1] + (SUBLANES, w.shape[-1]))


def _rmsnorm(x, w8):
    return _mul_rows(x * lax.rsqrt(jnp.mean(x * x, axis=-1, keepdims=True) + NORM_EPS), w8)


def _silu(x):
    return x / (1.0 + jnp.exp(-x))


def _inproj_kernel(x_ref, nw_ref, w_ref, wg_ref, cw_ref, qkv_ref, g_ref, z_ref, ba_ref, gbuf_ref):
    tm = x_ref.shape[0]
    hd = GDN_HEAD_DIM
    halo = SUBLANES
    o0, o1, o2, o3 = 0, ATTN_QKV_DIM, ATTN_QKV_DIM + GDN_QKV_DIM, IN_MAIN
    dot = functools.partial(jnp.dot, preferred_element_type=F32)

    nblk = GDN_QKV_DIM // hd

    @pl.when(pl.program_id(1) == 0)
    def _():
        gbuf_ref[:, 0:halo, :] = jnp.zeros((nblk, halo, hd), F32)

    h = _rmsnorm(x_ref[...], nw_ref[...]).astype(BF16)
    graw = dot(h, w_ref[:, o1:o2])
    for blk in range(nblk):
        gbuf_ref[blk, halo:halo + tm, :] = graw[:, blk * hd:(blk + 1) * hd]
    qkv_ref[...] = dot(h, w_ref[:, o0:o1]).astype(BF16)
    z_ref[...] = dot(h, w_ref[:, o2:o3]).astype(BF16)
    ba_ref[...] = dot(h, wg_ref[...])

    for blk in range(nblk):
        c0, c1 = blk * hd, (blk + 1) * hd
        acc = None
        for j in range(GDN_CONV):
            sh = GDN_CONV - 1 - j
            term = _mul_rows(gbuf_ref[blk, halo - sh:halo - sh + tm, :], cw_ref[j, :, c0:c1])
            acc = term if acc is None else acc + term
        y = _silu(acc)
        part = blk // GDN_HEADS
        if part < 2:
            y = y * lax.rsqrt(jnp.sum(y * y, axis=-1, keepdims=True) + NORM_EPS)
        if part == 0:
            y = y * (hd ** -0.5)
        g_ref[:, c0:c1] = y.astype(BF16)
    gbuf_ref[:, 0:halo, :] = gbuf_ref[:, tm:tm + halo, :]


def _inproj(x, norm_w8, w_main, w_gate, conv_w8, layer, batch, seq):
    m = x.shape[0]
    tm = min(IN_TM, seq)
    nt = seq // tm
    row = lambda b, t: (b * nt + t, 0)
    return pl.pallas_call(
        _inproj_kernel,
        grid=(batch, nt),
        in_specs=[pl.BlockSpec((tm, D_MODEL), row),
                  pl.BlockSpec((None, SUBLANES, D_MODEL), lambda b, t: (layer, 0, 0)),
                  pl.BlockSpec((None, D_MODEL, IN_MAIN), lambda b, t: (layer, 0, 0), pipeline_mode=pl.Buffered(1)),
                  pl.BlockSpec((None, D_MODEL, GATE_DIM), lambda b, t: (layer, 0, 0)),
                  pl.BlockSpec((None, GDN_CONV, SUBLANES, GDN_QKV_DIM), lambda b, t: (layer, 0, 0, 0))],
        out_specs=[pl.BlockSpec((tm, ATTN_QKV_DIM), row),
                   pl.BlockSpec((tm, GDN_QKV_DIM), row),
                   pl.BlockSpec((tm, GDN_DIM), row),
                   pl.BlockSpec((tm, GATE_DIM), row)],
        out_shape=[jax.ShapeDtypeStruct((m, ATTN_QKV_DIM), BF16),
                   jax.ShapeDtypeStruct((m, GDN_QKV_DIM), BF16),
                   jax.ShapeDtypeStruct((m, GDN_DIM), BF16),
                   jax.ShapeDtypeStruct((m, GATE_DIM), F32)],
        scratch_shapes=[pltpu.VMEM((GDN_QKV_DIM // GDN_HEAD_DIM, tm + SUBLANES, GDN_HEAD_DIM), F32)],
        compiler_params=pltpu.CompilerParams(dimension_semantics=("parallel", "arbitrary"),
                                             vmem_limit_bytes=VMEM_LIMIT),
        name="inproj",
    )(x, norm_w8, w_main, w_gate, conv_w8)


def _attn_bias():
    qpos = np.arange(WINDOW)[None, :] + WINDOW
    kpos = np.arange(2 * WINDOW)[:, None]
    rel = qpos - kpos
    band = (rel >= 0) & (rel < WINDOW)
    slopes = 2.0 ** (-8.0 * np.arange(1, ATTN_HEADS + 1) / ATTN_HEADS)
    alibi = -slopes[:, None, None] * rel[None].astype(np.float64)
    rest = np.where(band[None], alibi, NEG_BIG)
    first = np.where((band & (kpos >= WINDOW))[None], alibi, NEG_BIG)
    return jnp.asarray(np.stack([first, rest]), dtype=F32)


def _attn_steps(sink_ref, layer, q_ref, kvc_ref, kvp_ref, bias_ref, o_ref, kv_ref, first_tile):
    dh = ATTN_HEAD_DIM
    w = WINDOW
    nsub = q_ref.shape[0] // w
    nkv = kv_ref.shape[0]
    kv_ref[0:w, :] = kvp_ref[...]
    kv_ref[w:nkv, :] = kvc_ref[...]
    ones = jnp.ones((nkv, dh), F32)
    vt_ext = []
    for h in range(ATTN_KV_HEADS):
        vo = ATTN_KV_DIM + h * dh
        v_ext = jnp.concatenate([kv_ref[:, vo:vo + dh].astype(F32), ones], axis=1)
        vt_ext.append(v_ext.T.astype(BF16))
    yield
    for j in range(nsub):
        r0 = j * w
        bias_idx = jnp.where(first_tile, 0, 1) if j == 0 else 1
        for h in range(ATTN_KV_HEADS):
            k = kv_ref[r0:r0 + 2 * w, h * dh:(h + 1) * dh]
            vt = vt_ext[h][:, r0:r0 + 2 * w]
            for gp in range(ATTN_GROUP // 2):
                hq0 = h * ATTN_GROUP + 2 * gp
                res = []
                for hq in (hq0, hq0 + 1):
                    q = q_ref[r0:r0 + w, hq * dh:(hq + 1) * dh] * (dh ** -0.5)
                    s = _mm_nt(k, q) + bias_ref[bias_idx, hq]
                    sink = sink_ref[layer, hq]
                    m = jnp.maximum(jnp.max(s, axis=0, keepdims=True), sink)
                    p = jnp.exp(s - m).astype(BF16)
                    acc = jnp.dot(vt, p, preferred_element_type=F32)
                    den = acc[dh:dh + 1, :] + jnp.exp(sink - m)
                    res.append(acc[0:dh, :] / den)
                o_ref[r0:r0 + w, hq0 * dh:(hq0 + 2) * dh] = jnp.concatenate(res, axis=0).T.astype(BF16)
                yield


GDN_LEVELS = GDN_CHUNK.bit_length() - 1


def _gdn_chunk_local_steps(g_ref, ba_ref, gp_ref, staged):
    tt = g_ref.shape[0]
    nchunk = tt // GDN_CHUNK
    hd = GDN_HEAD_DIM
    heads = range(GDN_HEADS)
    dot = functools.partial(jnp.dot, preferred_element_type=F32)

    ba = ba_ref[...]
    beta_all = 1.0 / (1.0 + jnp.exp(-ba))
    pre = ba + gp_ref[1:2, :]
    softplus = jnp.maximum(pre, 0.0) + jnp.log(1.0 + jnp.exp(-jnp.abs(pre)))
    g_all = -jnp.exp(gp_ref[0:1, :]) * softplus

    ri = lax.broadcasted_iota(jnp.int32, (tt, tt), 0)
    ci = lax.broadcasted_iota(jnp.int32, (tt, tt), 1)
    causal = ((ri // GDN_CHUNK) == (ci // GDN_CHUNK)) & (ri >= ci)

    ltri = jnp.where(causal, 1.0, 0.0).astype(BF16)
    g1, g2, g3 = _split3(g_all)
    gc_all = (dot(ltri, g3) + dot(ltri, g2)) + dot(ltri, g1)
    gc_rows = gc_all.T
    gcl_all = jnp.concatenate(
        [jnp.broadcast_to(gc_all[(c + 1) * GDN_CHUNK - 1:(c + 1) * GDN_CHUNK, :], (GDN_CHUNK, GATE_DIM))
         for c in range(nchunk)], axis=0)
    egc_all = jnp.exp(gc_all)
    ekd_all = jnp.exp(gcl_all - gc_all)

    rc = lax.broadcasted_iota(jnp.int32, (GDN_CHUNK, GDN_CHUNK), 0)
    cc = lax.broadcasted_iota(jnp.int32, (GDN_CHUNK, GDN_CHUNK), 1)
    level_masks = []
    for lv in range(GDN_LEVELS):
        s = 1 << lv
        lower_left = ((rc // (2 * s)) == (cc // (2 * s))) & ((rc // s) > (cc // s))
        level_masks.append(jnp.where(lower_left, 1.0, 0.0).astype(BF16))
    eye = jnp.where(rc == cc, 1.0, 0.0)
    band = jnp.where(rc >= cc, 0.0, NEG_BIG)
    yield

    chunks = [slice(c * GDN_CHUNK, (c + 1) * GDN_CHUNK) for c in range(nchunk)]
    a_lv, qk_d, qd, kd, rhs = [], [], [], [], []
    for h in heads:
        q16 = g_ref[:, h * hd:(h + 1) * hd]
        k16 = g_ref[:, GDN_DIM + h * hd:GDN_DIM + (h + 1) * hd]
        q, k = q16.astype(F32), k16.astype(F32)
        v = g_ref[:, 2 * GDN_DIM + h * hd:2 * GDN_DIM + (h + 1) * hd].astype(F32)
        beta = beta_all[:, h:h + 1]
        gc = gc_all[:, GDN_HEADS + h:GDN_HEADS + h + 1]
        gc_row = gc_rows[GDN_HEADS + h:GDN_HEADS + h + 1, :]
        egc = egc_all[:, GDN_HEADS + h:GDN_HEADS + h + 1]
        kb = k * beta
        kb16 = kb.astype(BF16)
        a_h, qk_h = [], []
        for rs in chunks:
            decay = jnp.exp((gc[rs] - gc_row[:, rs]) + band)
            a_mat = (_mm_nt(kb16[rs], k16[rs]) * decay).astype(BF16)
            qk_h.append((_mm_nt(q16[rs], k16[rs]) * decay).astype(BF16))
            a_h.append([a_mat * m for m in level_masks])
        a_lv.append(a_h)
        qk_d.append(qk_h)
        qd.append((q * egc).astype(BF16))
        kd.append((k * ekd_all[:, GDN_HEADS + h:GDN_HEADS + h + 1]).astype(BF16))
        rhs.append(jnp.concatenate([v * beta, kb * egc], axis=1).astype(BF16))
        yield

    tinv = [[eye - a_lv[h][c][0].astype(F32) for c in range(nchunk)] for h in heads]
    for lv in range(1, GDN_LEVELS):
        for h in heads:
            for c in range(nchunk):
                t16 = tinv[h][c].astype(BF16)
                x = dot(a_lv[h][c][lv], t16).astype(BF16)
                tinv[h][c] = tinv[h][c] - dot(t16, x)
        yield
    for h in heads:
        sols = [dot(tinv[h][c].astype(BF16), rhs[h][chunks[c]]) for c in range(nchunk)]
        wq = [jnp.concatenate([sols[c][:, hd:2 * hd].astype(BF16), qd[h][chunks[c]]], axis=0) for c in range(nchunk)]
        staged.append((jnp.concatenate([sol[:, 0:hd] for sol in sols], axis=0), wq, qk_d[h], kd[h]))
    staged.append(egc_all)
    yield


GDN_LOCAL_STEPS = 1 + GDN_HEADS + (GDN_LEVELS - 1) + 1


def _gdn_recurrence_steps(su_ref, swq_ref, sqk_ref, skd_ref, segc_ref, z_ref, nw_ref, o_ref, state_ref):
    tt = su_ref.shape[1]
    nchunk = tt // GDN_CHUNK
    hd = GDN_HEAD_DIM
    heads = range(GDN_HEADS)
    dot = functools.partial(jnp.dot, preferred_element_type=F32)
    states = [state_ref[h] for h in heads]
    for c in range(nchunk):
        r0, r1 = c * GDN_CHUNK, (c + 1) * GDN_CHUNK
        for h in heads:
            ws = dot(swq_ref[h, c], states[h].astype(BF16))
            v_new = (su_ref[h, r0:r1, :] - ws[0:GDN_CHUNK]).astype(BF16)
            o = ws[GDN_CHUNK:2 * GDN_CHUNK] + dot(sqk_ref[h, c], v_new)
            gl = segc_ref[r1 - 1:r1, GDN_HEADS + h:GDN_HEADS + h + 1]
            states[h] = states[h] * gl + lax.dot_general(skd_ref[h, r0:r1, :], v_new, (((0,), (0,)), ((), ())),
                                                         preferred_element_type=F32)
            o = o * lax.rsqrt(jnp.mean(o * o, axis=-1, keepdims=True) + NORM_EPS) * nw_ref[...]
            zz = z_ref[r0:r1, h * hd:(h + 1) * hd].astype(F32)
            o_ref[r0:r1, h * hd:(h + 1) * hd] = (o * _silu(zz)).astype(BF16)
        if c == nchunk - 1:
            for h in heads:
                state_ref[h] = states[h]
        yield


def _mixer_kernel(sink_ref, q_ref, kvc_ref, kvp_ref, bias_ref, g_ref, ba_ref, gp_ref, z_ref, nw_ref,
                  attn_ref, gdn_ref, kv_ref, state_ref, su_ref, swq_ref, sqk_ref, skd_ref, segc_ref, *, tiles_per_seq, layer):
    i = pl.program_id(0)
    ntiles = pl.num_programs(0) - 1
    cur = jnp.minimum(i, ntiles - 1)
    prev = jnp.maximum(i - 1, 0)

    @pl.when(i == 0)
    def _():
        su_ref[...] = jnp.zeros_like(su_ref)
        swq_ref[...] = jnp.zeros_like(swq_ref)
        sqk_ref[...] = jnp.zeros_like(sqk_ref)
        skd_ref[...] = jnp.zeros_like(skd_ref)
        segc_ref[...] = jnp.zeros_like(segc_ref)

    @pl.when(prev % tiles_per_seq == 0)
    def _():
        state_ref[...] = jnp.zeros_like(state_ref)

    staged = []
    rec = _gdn_recurrence_steps(su_ref, swq_ref, sqk_ref, skd_ref, segc_ref, z_ref, nw_ref, gdn_ref, state_ref)
    loc = _gdn_chunk_local_steps(g_ref, ba_ref, gp_ref, staged)
    att = _attn_steps(sink_ref, layer, q_ref, kvc_ref, kvp_ref, bias_ref, attn_ref, kv_ref, cur % tiles_per_seq == 0)

    def advance(gen, n):
        for _ in range(n):
            next(gen, None)

    nrec = su_ref.shape[1] // GDN_CHUNK
    nloc = GDN_LOCAL_STEPS
    natt = 1 + (q_ref.shape[0] // WINDOW) * ATTN_KV_HEADS * (ATTN_GROUP // 2)
    for c in range(nrec):
        advance(rec, 1)
        advance(loc, -(-nloc // nrec))
        advance(att, -(-natt // nrec))
    advance(loc, nloc)
    advance(att, natt)

    for h in range(GDN_HEADS):
        u, wq, qk_d, kd = staged[h]
        su_ref[h] = u
        skd_ref[h] = kd
        for c in range(nrec):
            swq_ref[h, c] = wq[c]
            sqk_ref[h, c] = qk_d[c]
    segc_ref[...] = staged[GDN_HEADS]


def _gate_params(a_log, dt_bias):
    gp = jnp.zeros((a_log.shape[0], SUBLANES, GATE_DIM), F32)
    gp = gp.at[:, 0, GDN_HEADS:2 * GDN_HEADS].set(a_log.astype(F32))
    return gp.at[:, 1, GDN_HEADS:2 * GDN_HEADS].set(dt_bias.astype(F32))


def _mixers(qkv, sinks, bias, g, z, ba, gate_params, norm_w, layer, batch, seq):
    m = g.shape[0]
    tt = min(MIX_TT, seq)
    nt = seq // tt
    ntiles = batch * nt
    sub = tt // WINDOW
    nchunk = tt // GDN_CHUNK
    kv_col = ATTN_Q_DIM // (2 * ATTN_KV_DIM)
    cur = lambda i: (jnp.minimum(i, ntiles - 1), 0)
    prev = lambda i: (jnp.maximum(i - 1, 0), 0)
    hd = GDN_HEAD_DIM
    return pl.pallas_call(
        functools.partial(_mixer_kernel, tiles_per_seq=nt, layer=layer),
        grid=(ntiles + 1,),
        in_specs=[pl.BlockSpec(memory_space=pltpu.SMEM),
                  pl.BlockSpec((tt, ATTN_Q_DIM), cur),
                  pl.BlockSpec((tt, 2 * ATTN_KV_DIM), lambda i: (jnp.minimum(i, ntiles - 1), kv_col)),
                  pl.BlockSpec((WINDOW, 2 * ATTN_KV_DIM),
                               lambda i: (jnp.maximum(jnp.minimum(i, ntiles - 1) * sub - 1, 0), kv_col)),
                  pl.BlockSpec((2, ATTN_HEADS, 2 * WINDOW, WINDOW), lambda i: (0, 0, 0, 0)),
                  pl.BlockSpec((tt, GDN_QKV_DIM), cur),
                  pl.BlockSpec((tt, GATE_DIM), cur),
                  pl.BlockSpec((None, SUBLANES, GATE_DIM), lambda i: (layer, 0, 0)),
                  pl.BlockSpec((tt, GDN_DIM), prev),
                  pl.BlockSpec((None, 1, GDN_HEAD_DIM), lambda i: (layer, 0, 0))],
        out_specs=[pl.BlockSpec((tt, ATTN_Q_DIM), cur),
                   pl.BlockSpec((tt, GDN_DIM), prev)],
        out_shape=[jax.ShapeDtypeStruct((m, ATTN_Q_DIM), BF16),
                   jax.ShapeDtypeStruct((m, GDN_DIM), BF16)],
        scratch_shapes=[pltpu.VMEM((tt + WINDOW, 2 * ATTN_KV_DIM), BF16),
                        pltpu.VMEM((GDN_HEADS, hd, hd), F32),
                        pltpu.VMEM((GDN_HEADS, tt, hd), F32),
                        pltpu.VMEM((GDN_HEADS, nchunk, 2 * GDN_CHUNK, hd), BF16),
                        pltpu.VMEM((GDN_HEADS, nchunk, GDN_CHUNK, GDN_CHUNK), BF16),
                        pltpu.VMEM((GDN_HEADS, tt, hd), BF16),
                        pltpu.VMEM((tt, GATE_DIM), F32)],
        compiler_params=pltpu.CompilerParams(dimension_semantics=("arbitrary",),
                                             vmem_limit_bytes=VMEM_LIMIT),
        name="mixers",
    )(sinks, qkv, qkv, qkv, bias, g, ba, gate_params, z, norm_w)


def _mix_ffn_kernel(x_ref, attn_ref, gdn_ref, wout_ref, nw_ref, win_ref, cw_ref, cb_ref, wd_ref, fnw_ref,
                    o_ref, gbuf_ref, carry_ref, act_ref, *, final):
    tm = x_ref.shape[0]
    halo = SUBLANES

    @pl.when(pl.program_id(1) == 0)
    def _():
        carry_ref[...] = jnp.zeros_like(carry_ref)

    def rms(x, w):
        return x * lax.rsqrt(jnp.mean(x * x, axis=-1, keepdims=True) + NORM_EPS) * w

    dot = functools.partial(jnp.dot, preferred_element_type=F32)
    x1 = (x_ref[...] + dot(attn_ref[...], wout_ref[0:ATTN_Q_DIM, :])
          + dot(gdn_ref[...], wout_ref[ATTN_Q_DIM:MIX_DIM, :]))
    h = rms(x1, nw_ref[0:1, :]).astype(BF16)
    for c in range(D_FF // FFN_FC):
        c0, c1 = c * FFN_FC, (c + 1) * FFN_FC
        gate = dot(h, win_ref[:, c0:c1])
        up = dot(h, win_ref[:, D_FF + c0:D_FF + c1])
        gbuf_ref[0:halo, :] = carry_ref[:, c0:c1]
        gbuf_ref[halo:halo + tm, :] = gate
        carry_ref[:, c0:c1] = gate[tm - halo:tm, :]
        conv = gate * cw_ref[2, 0:1, c0:c1] + cb_ref[0:1, c0:c1]
        conv = conv + gbuf_ref[halo - 1:halo - 1 + tm, :] * cw_ref[1, 0:1, c0:c1]
        conv = conv + gbuf_ref[halo - 2:halo - 2 + tm, :] * cw_ref[0, 0:1, c0:c1]
        act_ref[:, c0:c1] = (_silu(conv) * up).astype(BF16)
    out = x1 + dot(act_ref[...], wd_ref[...])
    if final:
        out = rms(out, fnw_ref[0:1, :])
    o_ref[...] = out


def _mix_ffn(x, attn, gdn, w_out, norm_w8, w_ffn_in, conv_w8, conv_b8, w_down, final_norm8, layer, batch, seq, final):
    m = x.shape[0]
    tm = min(FFN_TM, seq)
    nt = seq // tm
    row = lambda b, t: (b * nt + t, 0)
    at_layer = lambda b, t: (layer, 0, 0)
    resident = lambda shape: pl.BlockSpec((None,) + shape, at_layer, pipeline_mode=pl.Buffered(1))
    return pl.pallas_call(
        functools.partial(_mix_ffn_kernel, final=final),
        grid=(batch, nt),
        in_specs=[pl.BlockSpec((tm, D_MODEL), row),
                  pl.BlockSpec((tm, ATTN_Q_DIM), row),
                  pl.BlockSpec((tm, GDN_DIM), row),
                  resident((MIX_DIM, D_MODEL)),
                  pl.BlockSpec((None, SUBLANES, D_MODEL), at_layer),
                  resident((D_MODEL, 2 * D_FF)),
                  pl.BlockSpec((None, FFN_CONV, SUBLANES, D_FF), lambda b, t: (layer, 0, 0, 0)),
                  pl.BlockSpec((None, SUBLANES, D_FF), at_layer),
                  resident((D_FF, D_MODEL)),
                  pl.BlockSpec((SUBLANES, D_MODEL), lambda b, t: (0, 0))],
        out_specs=pl.BlockSpec((tm, D_MODEL), row),
        out_shape=jax.ShapeDtypeStruct((m, D_MODEL), F32),
        scratch_shapes=[pltpu.VMEM((tm + SUBLANES, FFN_FC), F32),
                        pltpu.VMEM((SUBLANES, D_FF), F32),
                        pltpu.VMEM((tm, D_FF), BF16)],
        compiler_params=pltpu.CompilerParams(dimension_semantics=("parallel", "arbitrary"),
                                             vmem_limit_bytes=VMEM_LIMIT),
        name="mix_ffn",
    )(x, attn, gdn, w_out, norm_w8, w_ffn_in, conv_w8, conv_b8, w_down, final_norm8)


def kernel(x, attn_norm, w_in, attn_sinks, gdn_conv_w, gdn_a_log, gdn_dt_bias, gdn_norm, w_out, ffn_norm,
           w_ffn_in, ffn_conv_w, ffn_conv_b, w_down, final_norm):
    batch, seq, _ = x.shape
    assert seq % WINDOW == 0 and seq % GDN_CHUNK == 0
    xf = x.reshape(batch * seq, D_MODEL).astype(F32)
    bias = _attn_bias()
    depth = w_in.shape[0]
    w_main = w_in[:, :, :IN_MAIN].astype(BF16)
    w_gate = jnp.pad(w_in[:, :, IN_MAIN:], ((0, 0), (0, 0), (0, GATE_DIM - 2 * GDN_HEADS))).astype(BF16)
    w_out_b, w_ffn_in_b, w_down_b = w_out.astype(BF16), w_ffn_in.astype(BF16), w_down.astype(BF16)
    attn_norm8, gdn_conv8 = _rows8(attn_norm), _rows8(gdn_conv_w)
    ffn_norm8, ffn_conv8, ffn_bias8, final_norm8 = _rows8(ffn_norm), _rows8(ffn_conv_w), _rows8(ffn_conv_b), _rows8(final_norm)
    gate_params = _gate_params(gdn_a_log, gdn_dt_bias)
    sinks = attn_sinks.astype(F32)
    gdn_norm3 = gdn_norm.astype(F32)[:, None, :]
    for l in range(depth):
        qkv, g, z, ba = _inproj(xf, attn_norm8, w_main, w_gate, gdn_conv8, l, batch, seq)
        attn, gdn = _mixers(qkv, sinks, bias, g, z, ba, gate_params, gdn_norm3, l, batch, seq)
        xf = _mix_ffn(xf, attn, gdn, w_out_b, ffn_norm8, w_ffn_in_b, ffn_conv8, ffn_bias8, w_down_b, final_norm8,
                      l, batch, seq, final=(l == depth - 1))
    return xf.reshape(batch, seq, D_MODEL).astype(x.dtype)
```

```python
import functools

import numpy as np
import jax
import jax.numpy as jnp
from jax import lax
from jax.experimental import pallas as pl
from jax.experimental.pallas import tpu as pltpu

D_MODEL = 1024
ATTN_HEADS = 8
ATTN_KV_HEADS = 2
ATTN_GROUP = ATTN_HEADS // ATTN_KV_HEADS
ATTN_HEAD_DIM = 64
WINDOW = 128
GDN_HEADS = 4
GDN_HEAD_DIM = 128
GDN_CHUNK = 64
GDN_CONV = 4
D_FF = 2816
FFN_CONV = 3
NORM_EPS = 1e-6

ATTN_Q_DIM = ATTN_HEADS * ATTN_HEAD_DIM
ATTN_KV_DIM = ATTN_KV_HEADS * ATTN_HEAD_DIM
ATTN_QKV_DIM = ATTN_Q_DIM + 2 * ATTN_KV_DIM
GDN_DIM = GDN_HEADS * GDN_HEAD_DIM
GDN_QKV_DIM = 3 * GDN_DIM
MIX_DIM = ATTN_Q_DIM + GDN_DIM
IN_DIM = ATTN_QKV_DIM + GDN_QKV_DIM + GDN_DIM + 2 * GDN_HEADS
LANES = 128
SUBLANES = 8
GATE_DIM = LANES
IN_PAD = IN_DIM - 2 * GDN_HEADS + GATE_DIM

IN_TM = 1024
MIX_TT = 256
FFN_TM = 512
FFN_FC = 256
VMEM_LIMIT = 56 * 1024 * 1024

NEG_BIG = -1e30
BF16 = jnp.bfloat16
F32 = jnp.float32


def _mm_nt(a, b):
    return lax.dot_general(a.astype(BF16), b.astype(BF16), (((1,), (1,)), ((), ())),
                           preferred_element_type=F32)


def _split3(a):
    a1 = a.astype(BF16)
    r1 = a - a1.astype(F32)
    a2 = r1.astype(BF16)
    a3 = (r1 - a2.astype(F32)).astype(BF16)
    return a1, a2, a3


def _mul_rows(x, w8):
    n, d = x.shape
    return (x.reshape(n // SUBLANES, SUBLANES, d) * w8[None]).reshape(n, d)


def _rows8(w):
    w = w.astype(F32)
    return jnp.broadcast_to(w[..., None, :], w.shape[:-1] + (SUBLANES, w.shape[-1]))


def _rmsnorm(x, w8):
    return _mul_rows(x * lax.rsqrt(jnp.mean(x * x, axis=-1, keepdims=True) + NORM_EPS), w8)


def _silu(x):
    return x / (1.0 + jnp.exp(-x))


def _inproj_kernel(x_ref, nw_ref, w_ref, cw_ref, qkv_ref, g_ref, z_ref, ba_ref, gbuf_ref):
    tm = x_ref.shape[0]
    hd = GDN_HEAD_DIM
    halo = SUBLANES
    o0, o1, o2, o3 = 0, ATTN_QKV_DIM, ATTN_QKV_DIM + GDN_QKV_DIM, IN_PAD - GATE_DIM
    dot = functools.partial(jnp.dot, preferred_element_type=F32)

    nblk = GDN_QKV_DIM // hd

    @pl.when(pl.program_id(1) == 0)
    def _():
        gbuf_ref[:, 0:halo, :] = jnp.zeros((nblk, halo, hd), F32)

    h = _rmsnorm(x_ref[...], nw_ref[...]).astype(BF16)
    graw = dot(h, w_ref[:, o1:o2])
    for blk in range(nblk):
        gbuf_ref[blk, halo:halo + tm, :] = graw[:, blk * hd:(blk + 1) * hd]
    qkv_ref[...] = dot(h, w_ref[:, o0:o1]).astype(BF16)
    z_ref[...] = dot(h, w_ref[:, o2:o3]).astype(BF16)
    ba_ref[...] = dot(h, w_ref[:, o3:IN_PAD])

    for blk in range(nblk):
        c0, c1 = blk * hd, (blk + 1) * hd
        acc = None
        for j in range(GDN_CONV):
            sh = GDN_CONV - 1 - j
            term = _mul_rows(gbuf_ref[blk, halo - sh:halo - sh + tm, :], cw_ref[j, :, c0:c1])
            acc = term if acc is None else acc + term
        y = _silu(acc)
        part = blk // GDN_HEADS
        if part < 2:
            y = y * lax.rsqrt(jnp.sum(y * y, axis=-1, keepdims=True) + NORM_EPS)
        if part == 0:
            y = y * (hd ** -0.5)
        g_ref[:, c0:c1] = y.astype(BF16)
    gbuf_ref[:, 0:halo, :] = gbuf_ref[:, tm:tm + halo, :]


def _inproj(x, norm_w8, w_pad, conv_w8, layer, batch, seq):
    m = x.shape[0]
    tm = min(IN_TM, seq)
    nt = seq // tm
    row = lambda b, t: (b * nt + t, 0)
    return pl.pallas_call(
        _inproj_kernel,
        grid=(batch, nt),
        in_specs=[pl.BlockSpec((tm, D_MODEL), row),
                  pl.BlockSpec((None, SUBLANES, D_MODEL), lambda b, t: (layer, 0, 0)),
                  pl.BlockSpec((None, D_MODEL, IN_PAD), lambda b, t: (layer, 0, 0), pipeline_mode=pl.Buffered(1)),
                  pl.BlockSpec((None, GDN_CONV, SUBLANES, GDN_QKV_DIM), lambda b, t: (layer, 0, 0, 0))],
        out_specs=[pl.BlockSpec((tm, ATTN_QKV_DIM), row),
                   pl.BlockSpec((tm, GDN_QKV_DIM), row),
                   pl.BlockSpec((tm, GDN_DIM), row),
                   pl.BlockSpec((tm, GATE_DIM), row)],
        out_shape=[jax.ShapeDtypeStruct((m, ATTN_QKV_DIM), BF16),
                   jax.ShapeDtypeStruct((m, GDN_QKV_DIM), BF16),
                   jax.ShapeDtypeStruct((m, GDN_DIM), BF16),
                   jax.ShapeDtypeStruct((m, GATE_DIM), F32)],
        scratch_shapes=[pltpu.VMEM((GDN_QKV_DIM // GDN_HEAD_DIM, tm + SUBLANES, GDN_HEAD_DIM), F32)],
        compiler_params=pltpu.CompilerParams(dimension_semantics=("parallel", "arbitrary"),
                                             vmem_limit_bytes=VMEM_LIMIT),
        name="inproj",
    )(x, norm_w8, w_pad, conv_w8)


def _attn_bias():
    qpos = np.arange(WINDOW)[None, :] + WINDOW
    kpos = np.arange(2 * WINDOW)[:, None]
    rel = qpos - kpos
    band = (rel >= 0) & (rel < WINDOW)
    slopes = 2.0 ** (-8.0 * np.arange(1, ATTN_HEADS + 1) / ATTN_HEADS)
    alibi = -slopes[:, None, None] * rel[None].astype(np.float64)
    rest = np.where(band[None], alibi, NEG_BIG)
    first = np.where((band & (kpos >= WINDOW))[None], alibi, NEG_BIG)
    return jnp.asarray(np.stack([first, rest]), dtype=F32)


def _attn_steps(sink_ref, layer, q_ref, kvc_ref, kvp_ref, bias_ref, o_ref, kv_ref, first_tile):
    dh = ATTN_HEAD_DIM
    w = WINDOW
    nsub = q_ref.shape[0] // w
    nkv = kv_ref.shape[0]
    kv_ref[0:w, :] = kvp_ref[...]
    kv_ref[w:nkv, :] = kvc_ref[...]
    ones = jnp.ones((nkv, dh), F32)
    vt_ext = []
    for h in range(ATTN_KV_HEADS):
        vo = ATTN_KV_DIM + h * dh
        v_ext = jnp.concatenate([kv_ref[:, vo:vo + dh].astype(F32), ones], axis=1)
        vt_ext.append(v_ext.T.astype(BF16))
    yield
    for j in range(nsub):
        r0 = j * w
        bias_idx = jnp.where(first_tile, 0, 1) if j == 0 else 1
        for h in range(ATTN_KV_HEADS):
            k = kv_ref[r0:r0 + 2 * w, h * dh:(h + 1) * dh]
            vt = vt_ext[h][:, r0:r0 + 2 * w]
            for gp in range(ATTN_GROUP // 2):
                hq0 = h * ATTN_GROUP + 2 * gp
                res = []
                for hq in (hq0, hq0 + 1):
                    q = q_ref[r0:r0 + w, hq * dh:(hq + 1) * dh] * (dh ** -0.5)
                    s = _mm_nt(k, q) + bias_ref[bias_idx, hq]
                    sink = sink_ref[layer, hq]
                    m = jnp.maximum(jnp.max(s, axis=0, keepdims=True), sink)
                    p = jnp.exp(s - m).astype(BF16)
                    acc = jnp.dot(vt, p, preferred_element_type=F32)
                    den = acc[dh:dh + 1, :] + jnp.exp(sink - m)
                    res.append(acc[0:dh, :] / den)
                o_ref[r0:r0 + w, hq0 * dh:(hq0 + 2) * dh] = jnp.concatenate(res, axis=0).T.astype(BF16)
                yield


GDN_LEVELS = GDN_CHUNK.bit_length() - 1


def _gdn_chunk_local_steps(g_ref, ba_ref, gp_ref, staged):
    tt = g_ref.shape[0]
    nchunk = tt // GDN_CHUNK
    hd = GDN_HEAD_DIM
    heads = range(GDN_HEADS)
    dot = functools.partial(jnp.dot, preferred_element_type=F32)

    ba = ba_ref[...]
    beta_all = 1.0 / (1.0 + jnp.exp(-ba))
    pre = ba + gp_ref[1:2, :]
    softplus = jnp.maximum(pre, 0.0) + jnp.log(1.0 + jnp.exp(-jnp.abs(pre)))
    g_all = -jnp.exp(gp_ref[0:1, :]) * softplus

    ri = lax.broadcasted_iota(jnp.int32, (tt, tt), 0)
    ci = lax.broadcasted_iota(jnp.int32, (tt, tt), 1)
    causal = ((ri // GDN_CHUNK) == (ci // GDN_CHUNK)) & (ri >= ci)

    ltri = jnp.where(causal, 1.0, 0.0).astype(BF16)
    g1, g2, g3 = _split3(g_all)
    gc_all = (dot(ltri, g3) + dot(ltri, g2)) + dot(ltri, g1)
    gc_rows = gc_all.T
    gcl_all = jnp.concatenate(
        [jnp.broadcast_to(gc_all[(c + 1) * GDN_CHUNK - 1:(c + 1) * GDN_CHUNK, :], (GDN_CHUNK, GATE_DIM))
         for c in range(nchunk)], axis=0)
    egc_all = jnp.exp(gc_all)
    ekd_all = jnp.exp(gcl_all - gc_all)

    rc = lax.broadcasted_iota(jnp.int32, (GDN_CHUNK, GDN_CHUNK), 0)
    cc = lax.broadcasted_iota(jnp.int32, (GDN_CHUNK, GDN_CHUNK), 1)
    level_masks = []
    for lv in range(GDN_LEVELS):
        s = 1 << lv
        lower_left = ((rc // (2 * s)) == (cc // (2 * s))) & ((rc // s) > (cc // s))
        level_masks.append(jnp.where(lower_left, 1.0, 0.0).astype(BF16))
    eye = jnp.where(rc == cc, 1.0, 0.0)
    band = jnp.where(rc >= cc, 0.0, NEG_BIG)
    yield

    chunks = [slice(c * GDN_CHUNK, (c + 1) * GDN_CHUNK) for c in range(nchunk)]
    a_lv, qk_d, qd, kd, rhs = [], [], [], [], []
    for h in heads:
        q16 = g_ref[:, h * hd:(h + 1) * hd]
        k16 = g_ref[:, GDN_DIM + h * hd:GDN_DIM + (h + 1) * hd]
        q, k = q16.astype(F32), k16.astype(F32)
        v = g_ref[:, 2 * GDN_DIM + h * hd:2 * GDN_DIM + (h + 1) * hd].astype(F32)
        beta = beta_all[:, h:h + 1]
        gc = gc_all[:, GDN_HEADS + h:GDN_HEADS + h + 1]
        gc_row = gc_rows[GDN_HEADS + h:GDN_HEADS + h + 1, :]
        egc = egc_all[:, GDN_HEADS + h:GDN_HEADS + h + 1]
        kb = k * beta
        kb16 = kb.astype(BF16)
        a_h, qk_h = [], []
        for rs in chunks:
            decay = jnp.exp((gc[rs] - gc_row[:, rs]) + band)
            a_mat = (_mm_nt(kb16[rs], k16[rs]) * decay).astype(BF16)
            qk_h.append((_mm_nt(q16[rs], k16[rs]) * decay).astype(BF16))
            a_h.append([a_mat * m for m in level_masks])
        a_lv.append(a_h)
        qk_d.append(qk_h)
        qd.append((q * egc).astype(BF16))
        kd.append((k * ekd_all[:, GDN_HEADS + h:GDN_HEADS + h + 1]).astype(BF16))
        rhs.append(jnp.concatenate([v * beta, kb * egc], axis=1).astype(BF16))
        yield

    tinv = [[eye - a_lv[h][c][0].astype(F32) for c in range(nchunk)] for h in heads]
    for lv in range(1, GDN_LEVELS):
        for h in heads:
            for c in range(nchunk):
                t16 = tinv[h][c].astype(BF16)
                x = dot(a_lv[h][c][lv], t16).astype(BF16)
                tinv[h][c] = tinv[h][c] - dot(t16, x)
        yield
    for h in heads:
        sols = [dot(tinv[h][c].astype(BF16), rhs[h][chunks[c]]) for c in range(nchunk)]
        wq = [jnp.concatenate([sols[c][:, hd:2 * hd].astype(BF16), qd[h][chunks[c]]], axis=0) for c in range(nchunk)]
        staged.append((jnp.concatenate([sol[:, 0:hd] for sol in sols], axis=0), wq, qk_d[h], kd[h]))
    staged.append(egc_all)
    yield


GDN_LOCAL_STEPS = 1 + GDN_HEADS + (GDN_LEVELS - 1) + 1


def _gdn_recurrence_steps(su_ref, swq_ref, sqk_ref, skd_ref, segc_ref, z_ref, nw_ref, o_ref, state_ref):
    tt = su_ref.shape[1]
    nchunk = tt // GDN_CHUNK
    hd = GDN_HEAD_DIM
    heads = range(GDN_HEADS)
    dot = functools.partial(jnp.dot, preferred_element_type=F32)
    states = [state_ref[h] for h in heads]
    for c in range(nchunk):
        r0, r1 = c * GDN_CHUNK, (c + 1) * GDN_CHUNK
        for h in heads:
            ws = dot(swq_ref[h, c], states[h].astype(BF16))
            v_new = (su_ref[h, r0:r1, :] - ws[0:GDN_CHUNK]).astype(BF16)
            o = ws[GDN_CHUNK:2 * GDN_CHUNK] + dot(sqk_ref[h, c], v_new)
            gl = segc_ref[r1 - 1:r1, GDN_HEADS + h:GDN_HEADS + h + 1]
            states[h] = states[h] * gl + lax.dot_general(skd_ref[h, r0:r1, :], v_new, (((0,), (0,)), ((), ())),
                                                         preferred_element_type=F32)
            o = o * lax.rsqrt(jnp.mean(o * o, axis=-1, keepdims=True) + NORM_EPS) * nw_ref[...]
            zz = z_ref[r0:r1, h * hd:(h + 1) * hd].astype(F32)
            o_ref[r0:r1, h * hd:(h + 1) * hd] = (o * _silu(zz)).astype(BF16)
        if c == nchunk - 1:
            for h in heads:
                state_ref[h] = states[h]
        yield


def _mixer_kernel(sink_ref, q_ref, kvc_ref, kvp_ref, bias_ref, g_ref, ba_ref, gp_ref, z_ref, nw_ref,
                  attn_ref, gdn_ref, kv_ref, state_ref, su_ref, swq_ref, sqk_ref, skd_ref, segc_ref, *, tiles_per_seq, layer):
    i = pl.program_id(0)
    ntiles = pl.num_programs(0) - 1
    cur = jnp.minimum(i, ntiles - 1)
    prev = jnp.maximum(i - 1, 0)

    @pl.when(i == 0)
    def _():
        su_ref[...] = jnp.zeros_like(su_ref)
        swq_ref[...] = jnp.zeros_like(swq_ref)
        sqk_ref[...] = jnp.zeros_like(sqk_ref)
        skd_ref[...] = jnp.zeros_like(skd_ref)
        segc_ref[...] = jnp.zeros_like(segc_ref)

    @pl.when(prev % tiles_per_seq == 0)
    def _():
        state_ref[...] = jnp.zeros_like(state_ref)

    staged = []
    rec = _gdn_recurrence_steps(su_ref, swq_ref, sqk_ref, skd_ref, segc_ref, z_ref, nw_ref, gdn_ref, state_ref)
    loc = _gdn_chunk_local_steps(g_ref, ba_ref, gp_ref, staged)
    att = _attn_steps(sink_ref, layer, q_ref, kvc_ref, kvp_ref, bias_ref, attn_ref, kv_ref, cur % tiles_per_seq == 0)

    def advance(gen, n):
        for _ in range(n):
            next(gen, None)

    nrec = su_ref.shape[1] // GDN_CHUNK
    nloc = GDN_LOCAL_STEPS
    natt = 1 + (q_ref.shape[0] // WINDOW) * ATTN_KV_HEADS * (ATTN_GROUP // 2)
    for c in range(nrec):
        advance(rec, 1)
        advance(loc, -(-nloc // nrec))
        advance(att, -(-natt // nrec))
    advance(loc, nloc)
    advance(att, natt)

    for h in range(GDN_HEADS):
        u, wq, qk_d, kd = staged[h]
        su_ref[h] = u
        skd_ref[h] = kd
        for c in range(nrec):
            swq_ref[h, c] = wq[c]
            sqk_ref[h, c] = qk_d[c]
    segc_ref[...] = staged[GDN_HEADS]


def _gate_params(a_log, dt_bias):
    gp = jnp.zeros((a_log.shape[0], SUBLANES, GATE_DIM), F32)
    gp = gp.at[:, 0, GDN_HEADS:2 * GDN_HEADS].set(a_log.astype(F32))
    return gp.at[:, 1, GDN_HEADS:2 * GDN_HEADS].set(dt_bias.astype(F32))


def _mixers(qkv, sinks, bias, g, z, ba, gate_params, norm_w, layer, batch, seq):
    m = g.shape[0]
    tt = min(MIX_TT, seq)
    nt = seq // tt
    ntiles = batch * nt
    sub = tt // WINDOW
    nchunk = tt // GDN_CHUNK
    kv_col = ATTN_Q_DIM // (2 * ATTN_KV_DIM)
    cur = lambda i: (jnp.minimum(i, ntiles - 1), 0)
    prev = lambda i: (jnp.maximum(i - 1, 0), 0)
    hd = GDN_HEAD_DIM
    return pl.pallas_call(
        functools.partial(_mixer_kernel, tiles_per_seq=nt, layer=layer),
        grid=(ntiles + 1,),
        in_specs=[pl.BlockSpec(memory_space=pltpu.SMEM),
                  pl.BlockSpec((tt, ATTN_Q_DIM), cur),
                  pl.BlockSpec((tt, 2 * ATTN_KV_DIM), lambda i: (jnp.minimum(i, ntiles - 1), kv_col)),
                  pl.BlockSpec((WINDOW, 2 * ATTN_KV_DIM),
                               lambda i: (jnp.maximum(jnp.minimum(i, ntiles - 1) * sub - 1, 0), kv_col)),
                  pl.BlockSpec((2, ATTN_HEADS, 2 * WINDOW, WINDOW), lambda i: (0, 0, 0, 0)),
                  pl.BlockSpec((tt, GDN_QKV_DIM), cur),
                  pl.BlockSpec((tt, GATE_DIM), cur),
                  pl.BlockSpec((None, SUBLANES, GATE_DIM), lambda i: (layer, 0, 0)),
                  pl.BlockSpec((tt, GDN_DIM), prev),
                  pl.BlockSpec((None, 1, GDN_HEAD_DIM), lambda i: (layer, 0, 0))],
        out_specs=[pl.BlockSpec((tt, ATTN_Q_DIM), cur),
                   pl.BlockSpec((tt, GDN_DIM), prev)],
        out_shape=[jax.ShapeDtypeStruct((m, ATTN_Q_DIM), BF16),
                   jax.ShapeDtypeStruct((m, GDN_DIM), BF16)],
        scratch_shapes=[pltpu.VMEM((tt + WINDOW, 2 * ATTN_KV_DIM), BF16),
                        pltpu.VMEM((GDN_HEADS, hd, hd), F32),
                        pltpu.VMEM((GDN_HEADS, tt, hd), F32),
                        pltpu.VMEM((GDN_HEADS, nchunk, 2 * GDN_CHUNK, hd), BF16),
                        pltpu.VMEM((GDN_HEADS, nchunk, GDN_CHUNK, GDN_CHUNK), BF16),
                        pltpu.VMEM((GDN_HEADS, tt, hd), BF16),
                        pltpu.VMEM((tt, GATE_DIM), F32)],
        compiler_params=pltpu.CompilerParams(dimension_semantics=("arbitrary",),
                                             vmem_limit_bytes=VMEM_LIMIT),
        name="mixers",
    )(sinks, qkv, qkv, qkv, bias, g, ba, gate_params, z, norm_w)


def _mix_ffn_kernel(x_ref, attn_ref, gdn_ref, wout_ref, nw_ref, win_ref, cw_ref, cb_ref, wd_ref, fnw_ref,
                    o_ref, gbuf_ref, carry_ref, act_ref, *, final):
    tm = x_ref.shape[0]
    halo = SUBLANES

    @pl.when(pl.program_id(1) == 0)
    def _():
        carry_ref[...] = jnp.zeros_like(carry_ref)

    def rms(x, w):
        return x * lax.rsqrt(jnp.mean(x * x, axis=-1, keepdims=True) + NORM_EPS) * w

    dot = functools.partial(jnp.dot, preferred_element_type=F32)
    x1 = (x_ref[...] + dot(attn_ref[...], wout_ref[0:ATTN_Q_DIM, :])
          + dot(gdn_ref[...], wout_ref[ATTN_Q_DIM:MIX_DIM, :]))
    h = rms(x1, nw_ref[0:1, :]).astype(BF16)
    for c in range(D_FF // FFN_FC):
        c0, c1 = c * FFN_FC, (c + 1) * FFN_FC
        gate = dot(h, win_ref[:, c0:c1])
        up = dot(h, win_ref[:, D_FF + c0:D_FF + c1])
        gbuf_ref[0:halo, :] = carry_ref[:, c0:c1]
        gbuf_ref[halo:halo + tm, :] = gate
        carry_ref[:, c0:c1] = gate[tm - halo:tm, :]
        conv = gate * cw_ref[2, 0:1, c0:c1] + cb_ref[0:1, c0:c1]
        conv = conv + gbuf_ref[halo - 1:halo - 1 + tm, :] * cw_ref[1, 0:1, c0:c1]
        conv = conv + gbuf_ref[halo - 2:halo - 2 + tm, :] * cw_ref[0, 0:1, c0:c1]
        act_ref[:, c0:c1] = (_silu(conv) * up).astype(BF16)
    out = x1 + dot(act_ref[...], wd_ref[...])
    if final:
        out = rms(out, fnw_ref[0:1, :])
    o_ref[...] = out


def _mix_ffn(x, attn, gdn, w_out, norm_w8, w_ffn_in, conv_w8, conv_b8, w_down, final_norm8, layer, batch, seq, final):
    m = x.shape[0]
    tm = min(FFN_TM, seq)
    nt = seq // tm
    row = lambda b, t: (b * nt + t, 0)
    at_layer = lambda b, t: (layer, 0, 0)
    resident = lambda shape: pl.BlockSpec((None,) + shape, at_layer, pipeline_mode=pl.Buffered(1))
    return pl.pallas_call(
        functools.partial(_mix_ffn_kernel, final=final),
        grid=(batch, nt),
        in_specs=[pl.BlockSpec((tm, D_MODEL), row),
                  pl.BlockSpec((tm, ATTN_Q_DIM), row),
                  pl.BlockSpec((tm, GDN_DIM), row),
                  resident((MIX_DIM, D_MODEL)),
                  pl.BlockSpec((None, SUBLANES, D_MODEL), at_layer),
                  resident((D_MODEL, 2 * D_FF)),
                  pl.BlockSpec((None, FFN_CONV, SUBLANES, D_FF), lambda b, t: (layer, 0, 0, 0)),
                  pl.BlockSpec((None, SUBLANES, D_FF), at_layer),
                  resident((D_FF, D_MODEL)),
                  pl.BlockSpec((SUBLANES, D_MODEL), lambda b, t: (0, 0))],
        out_specs=pl.BlockSpec((tm, D_MODEL), row),
        out_shape=jax.ShapeDtypeStruct((m, D_MODEL), F32),
        scratch_shapes=[pltpu.VMEM((tm + SUBLANES, FFN_FC), F32),
                        pltpu.VMEM((SUBLANES, D_FF), F32),
                        pltpu.VMEM((tm, D_FF), BF16)],
        compiler_params=pltpu.CompilerParams(dimension_semantics=("parallel", "arbitrary"),
                                             vmem_limit_bytes=VMEM_LIMIT),
        name="mix_ffn",
    )(x, attn, gdn, w_out, norm_w8, w_ffn_in, conv_w8, conv_b8, w_down, final_norm8)


def kernel(x, attn_norm, w_in, attn_sinks, gdn_conv_w, gdn_a_log, gdn_dt_bias, gdn_norm, w_out, ffn_norm,
           w_ffn_in, ffn_conv_w, ffn_conv_b, w_down, final_norm):
    batch, seq, _ = x.shape
    assert seq % WINDOW == 0 and seq % GDN_CHUNK == 0
    xf = x.reshape(batch * seq, D_MODEL).astype(F32)
    bias = _attn_bias()
    depth = w_in.shape[0]
    w_in_b = jnp.pad(w_in, ((0, 0), (0, 0), (0, IN_PAD - IN_DIM))).astype(BF16)
    w_out_b, w_ffn_in_b, w_down_b = w_out.astype(BF16), w_ffn_in.astype(BF16), w_down.astype(BF16)
    attn_norm8, gdn_conv8 = _rows8(attn_norm), _rows8(gdn_conv_w)
    ffn_norm8, ffn_conv8, ffn_bias8, final_norm8 = _rows8(ffn_norm), _rows8(ffn_conv_w), _rows8(ffn_conv_b), _rows8(final_norm)
    gate_params = _gate_params(gdn_a_log, gdn_dt_bias)
    sinks = attn_sinks.astype(F32)
    gdn_norm3 = gdn_norm.astype(F32)[:, None, :]
    for l in range(depth):
        qkv, g, z, ba = _inproj(xf, attn_norm8, w_in_b, gdn_conv8, l, batch, seq)
        attn, gdn = _mixers(qkv, sinks, bias, g, z, ba, gate_params, gdn_norm3, l, batch, seq)
        xf = _mix_ffn(xf, attn, gdn, w_out_b, ffn_norm8, w_ffn_in_b, ffn_conv8, ffn_bias8, w_down_b, final_norm8,
                      l, batch, seq, final=(l == depth - 1))
    return xf.reshape(batch, seq, D_MODEL).astype(x.dtype)
```

```python
import functools

import numpy as np
import jax
import jax.numpy as jnp
from jax import lax
from jax.experimental import pallas as pl
from jax.experimental.pallas import tpu as pltpu

D_MODEL = 1024
ATTN_HEADS = 8
ATTN_KV_HEADS = 2
ATTN_GROUP = ATTN_HEADS // ATTN_KV_HEADS
ATTN_HEAD_DIM = 64
WINDOW = 128
GDN_HEADS = 4
GDN_HEAD_DIM = 128
GDN_CHUNK = 64
GDN_CONV = 4
D_FF = 2816
FFN_CONV = 3
NORM_EPS = 1e-6

ATTN_Q_DIM = ATTN_HEADS * ATTN_HEAD_DIM
ATTN_KV_DIM = ATTN_KV_HEADS * ATTN_HEAD_DIM
ATTN_QKV_DIM = ATTN_Q_DIM + 2 * ATTN_KV_DIM
GDN_DIM = GDN_HEADS * GDN_HEAD_DIM
GDN_QKV_DIM = 3 * GDN_DIM
MIX_DIM = ATTN_Q_DIM + GDN_DIM
IN_DIM = ATTN_QKV_DIM + GDN_QKV_DIM + GDN_DIM + 2 * GDN_HEADS
LANES = 128
SUBLANES = 8
GATE_DIM = LANES
IN_PAD = IN_DIM - 2 * GDN_HEADS + GATE_DIM

IN_TM = 1024
MIX_TT = 256
FFN_TM = 1024
FFN_FC = 256
VMEM_LIMIT = 60 * 1024 * 1024

NEG_BIG = -1e30
BF16 = jnp.bfloat16
F32 = jnp.float32


def _mm_nt(a, b):
    return lax.dot_general(a.astype(BF16), b.astype(BF16), (((1,), (1,)), ((), ())),
                           preferred_element_type=F32)


def _split3(a):
    a1 = a.astype(BF16)
    r1 = a - a1.astype(F32)
    a2 = r1.astype(BF16)
    a3 = (r1 - a2.astype(F32)).astype(BF16)
    return a1, a2, a3


def _mul_rows(x, w8):
    n, d = x.shape
    return (x.reshape(n // SUBLANES, SUBLANES, d) * w8[None]).reshape(n, d)


def _rows8(w):
    w = w.astype(F32)
    return jnp.broadcast_to(w[..., None, :], w.shape[:-1] + (SUBLANES, w.shape[-1]))


def _rmsnorm(x, w8):
    return _mul_rows(x * lax.rsqrt(jnp.mean(x * x, axis=-1, keepdims=True) + NORM_EPS), w8)


def _silu(x):
    return x / (1.0 + jnp.exp(-x))


def _inproj_kernel(x_ref, nw_ref, w_ref, cw_ref, qkv_ref, g_ref, z_ref, ba_ref, gbuf_ref):
    tm = x_ref.shape[0]
    hd = GDN_HEAD_DIM
    halo = SUBLANES
    o0, o1, o2, o3 = 0, ATTN_QKV_DIM, ATTN_QKV_DIM + GDN_QKV_DIM, IN_PAD - GATE_DIM
    dot = functools.partial(jnp.dot, preferred_element_type=F32)

    nblk = GDN_QKV_DIM // hd

    @pl.when(pl.program_id(1) == 0)
    def _():
        gbuf_ref[:, 0:halo, :] = jnp.zeros((nblk, halo, hd), F32)

    h = _rmsnorm(x_ref[...], nw_ref[...]).astype(BF16)
    graw = dot(h, w_ref[:, o1:o2])
    for blk in range(nblk):
        gbuf_ref[blk, halo:halo + tm, :] = graw[:, blk * hd:(blk + 1) * hd]
    qkv_ref[...] = dot(h, w_ref[:, o0:o1]).astype(BF16)
    z_ref[...] = dot(h, w_ref[:, o2:o3]).astype(BF16)
    ba_ref[...] = dot(h, w_ref[:, o3:IN_PAD])

    for blk in range(nblk):
        c0, c1 = blk * hd, (blk + 1) * hd
        acc = None
        for j in range(GDN_CONV):
            sh = GDN_CONV - 1 - j
            term = _mul_rows(gbuf_ref[blk, halo - sh:halo - sh + tm, :], cw_ref[j, :, c0:c1])
            acc = term if acc is None else acc + term
        y = _silu(acc)
        part = blk // GDN_HEADS
        if part < 2:
            y = y * lax.rsqrt(jnp.sum(y * y, axis=-1, keepdims=True) + NORM_EPS)
        if part == 0:
            y = y * (hd ** -0.5)
        g_ref[:, c0:c1] = y.astype(BF16)
    gbuf_ref[:, 0:halo, :] = gbuf_ref[:, tm:tm + halo, :]


def _inproj(x, norm_w8, w_pad, conv_w8, layer, batch, seq):
    m = x.shape[0]
    tm = min(IN_TM, seq)
    nt = seq // tm
    row = lambda b, t: (b * nt + t, 0)
    return pl.pallas_call(
        _inproj_kernel,
        grid=(batch, nt),
        in_specs=[pl.BlockSpec((tm, D_MODEL), row),
                  pl.BlockSpec((None, SUBLANES, D_MODEL), lambda b, t: (layer, 0, 0)),
                  pl.BlockSpec((None, D_MODEL, IN_PAD), lambda b, t: (layer, 0, 0), pipeline_mode=pl.Buffered(1)),
                  pl.BlockSpec((None, GDN_CONV, SUBLANES, GDN_QKV_DIM), lambda b, t: (layer, 0, 0, 0))],
        out_specs=[pl.BlockSpec((tm, ATTN_QKV_DIM), row),
                   pl.BlockSpec((tm, GDN_QKV_DIM), row),
                   pl.BlockSpec((tm, GDN_DIM), row),
                   pl.BlockSpec((tm, GATE_DIM), row)],
        out_shape=[jax.ShapeDtypeStruct((m, ATTN_QKV_DIM), BF16),
                   jax.ShapeDtypeStruct((m, GDN_QKV_DIM), BF16),
                   jax.ShapeDtypeStruct((m, GDN_DIM), BF16),
                   jax.ShapeDtypeStruct((m, GATE_DIM), F32)],
        scratch_shapes=[pltpu.VMEM((GDN_QKV_DIM // GDN_HEAD_DIM, tm + SUBLANES, GDN_HEAD_DIM), F32)],
        compiler_params=pltpu.CompilerParams(dimension_semantics=("parallel", "arbitrary"),
                                             vmem_limit_bytes=VMEM_LIMIT),
        name="inproj",
    )(x, norm_w8, w_pad, conv_w8)


def _attn_bias():
    qpos = np.arange(WINDOW)[None, :] + WINDOW
    kpos = np.arange(2 * WINDOW)[:, None]
    rel = qpos - kpos
    band = (rel >= 0) & (rel < WINDOW)
    slopes = 2.0 ** (-8.0 * np.arange(1, ATTN_HEADS + 1) / ATTN_HEADS)
    alibi = -slopes[:, None, None] * rel[None].astype(np.float64)
    rest = np.where(band[None], alibi, NEG_BIG)
    first = np.where((band & (kpos >= WINDOW))[None], alibi, NEG_BIG)
    return jnp.asarray(np.stack([first, rest]), dtype=F32)


def _attn_steps(sink_ref, layer, q_ref, kvc_ref, kvp_ref, bias_ref, o_ref, kv_ref, first_tile):
    dh = ATTN_HEAD_DIM
    w = WINDOW
    nsub = q_ref.shape[0] // w
    nkv = kv_ref.shape[0]
    kv_ref[0:w, :] = kvp_ref[...]
    kv_ref[w:nkv, :] = kvc_ref[...]
    ones = jnp.ones((nkv, dh), F32)
    vt_ext = []
    for h in range(ATTN_KV_HEADS):
        vo = ATTN_KV_DIM + h * dh
        v_ext = jnp.concatenate([kv_ref[:, vo:vo + dh].astype(F32), ones], axis=1)
        vt_ext.append(v_ext.T.astype(BF16))
    yield
    for j in range(nsub):
        r0 = j * w
        bias_idx = jnp.where(first_tile, 0, 1) if j == 0 else 1
        for h in range(ATTN_KV_HEADS):
            k = kv_ref[r0:r0 + 2 * w, h * dh:(h + 1) * dh]
            vt = vt_ext[h][:, r0:r0 + 2 * w]
            for gp in range(ATTN_GROUP // 2):
                hq0 = h * ATTN_GROUP + 2 * gp
                res = []
                for hq in (hq0, hq0 + 1):
                    q = q_ref[r0:r0 + w, hq * dh:(hq + 1) * dh] * (dh ** -0.5)
                    s = _mm_nt(k, q) + bias_ref[bias_idx, hq]
                    sink = sink_ref[layer, hq]
                    m = jnp.maximum(jnp.max(s, axis=0, keepdims=True), sink)
                    p = jnp.exp(s - m).astype(BF16)
                    acc = jnp.dot(vt, p, preferred_element_type=F32)
                    den = acc[dh:dh + 1, :] + jnp.exp(sink - m)
                    res.append(acc[0:dh, :] / den)
                o_ref[r0:r0 + w, hq0 * dh:(hq0 + 2) * dh] = jnp.concatenate(res, axis=0).T.astype(BF16)
                yield


GDN_LEVELS = GDN_CHUNK.bit_length() - 1


def _gdn_chunk_local_steps(g_ref, ba_ref, gp_ref, staged):
    tt = g_ref.shape[0]
    nchunk = tt // GDN_CHUNK
    hd = GDN_HEAD_DIM
    heads = range(GDN_HEADS)
    dot = functools.partial(jnp.dot, preferred_element_type=F32)

    ba = ba_ref[...]
    beta_all = 1.0 / (1.0 + jnp.exp(-ba))
    pre = ba + gp_ref[1:2, :]
    softplus = jnp.maximum(pre, 0.0) + jnp.log(1.0 + jnp.exp(-jnp.abs(pre)))
    g_all = -jnp.exp(gp_ref[0:1, :]) * softplus

    ri = lax.broadcasted_iota(jnp.int32, (tt, tt), 0)
    ci = lax.broadcasted_iota(jnp.int32, (tt, tt), 1)
    causal = ((ri // GDN_CHUNK) == (ci // GDN_CHUNK)) & (ri >= ci)

    ltri = jnp.where(causal, 1.0, 0.0).astype(BF16)
    g1, g2, g3 = _split3(g_all)
    gc_all = (dot(ltri, g3) + dot(ltri, g2)) + dot(ltri, g1)
    gc_rows = gc_all.T
    gcl_all = jnp.concatenate(
        [jnp.broadcast_to(gc_all[(c + 1) * GDN_CHUNK - 1:(c + 1) * GDN_CHUNK, :], (GDN_CHUNK, GATE_DIM))
         for c in range(nchunk)], axis=0)
    egc_all = jnp.exp(gc_all)
    ekd_all = jnp.exp(gcl_all - gc_all)

    rc = lax.broadcasted_iota(jnp.int32, (GDN_CHUNK, GDN_CHUNK), 0)
    cc = lax.broadcasted_iota(jnp.int32, (GDN_CHUNK, GDN_CHUNK), 1)
    level_masks = []
    for lv in range(GDN_LEVELS):
        s = 1 << lv
        lower_left = ((rc // (2 * s)) == (cc // (2 * s))) & ((rc // s) > (cc // s))
        level_masks.append(jnp.where(lower_left, 1.0, 0.0).astype(BF16))
    eye = jnp.where(rc == cc, 1.0, 0.0)
    band = jnp.where(rc >= cc, 0.0, NEG_BIG)
    yield

    chunks = [slice(c * GDN_CHUNK, (c + 1) * GDN_CHUNK) for c in range(nchunk)]
    a_lv, qk_d, qd, kd, rhs = [], [], [], [], []
    for h in heads:
        q16 = g_ref[:, h * hd:(h + 1) * hd]
        k16 = g_ref[:, GDN_DIM + h * hd:GDN_DIM + (h + 1) * hd]
        q, k = q16.astype(F32), k16.astype(F32)
        v = g_ref[:, 2 * GDN_DIM + h * hd:2 * GDN_DIM + (h + 1) * hd].astype(F32)
        beta = beta_all[:, h:h + 1]
        gc = gc_all[:, GDN_HEADS + h:GDN_HEADS + h + 1]
        gc_row = gc_rows[GDN_HEADS + h:GDN_HEADS + h + 1, :]
        egc = egc_all[:, GDN_HEADS + h:GDN_HEADS + h + 1]
        kb = k * beta
        kb16 = kb.astype(BF16)
        a_h, qk_h = [], []
        for rs in chunks:
            decay = jnp.exp((gc[rs] - gc_row[:, rs]) + band)
            a_mat = (_mm_nt(kb16[rs], k16[rs]) * decay).astype(BF16)
            qk_h.append((_mm_nt(q16[rs], k16[rs]) * decay).astype(BF16))
            a_h.append([a_mat * m for m in level_masks])
        a_lv.append(a_h)
        qk_d.append(qk_h)
        qd.append((q * egc).astype(BF16))
        kd.append((k * ekd_all[:, GDN_HEADS + h:GDN_HEADS + h + 1]).astype(BF16))
        rhs.append(jnp.concatenate([v * beta, kb * egc], axis=1).astype(BF16))
        yield

    tinv = [[eye - a_lv[h][c][0].astype(F32) for c in range(nchunk)] for h in heads]
    for lv in range(1, GDN_LEVELS):
        for h in heads:
            for c in range(nchunk):
                t16 = tinv[h][c].astype(BF16)
                x = dot(a_lv[h][c][lv], t16).astype(BF16)
                tinv[h][c] = tinv[h][c] - dot(t16, x)
        yield
    for h in heads:
        sols = [dot(tinv[h][c].astype(BF16), rhs[h][chunks[c]]) for c in range(nchunk)]
        wq = [jnp.concatenate([sols[c][:, hd:2 * hd].astype(BF16), qd[h][chunks[c]]], axis=0) for c in range(nchunk)]
        staged.append((jnp.concatenate([sol[:, 0:hd] for sol in sols], axis=0), wq, qk_d[h], kd[h]))
    staged.append(egc_all)
    yield


GDN_LOCAL_STEPS = 1 + GDN_HEADS + (GDN_LEVELS - 1) + 1


def _gdn_recurrence_steps(su_ref, swq_ref, sqk_ref, skd_ref, segc_ref, z_ref, nw_ref, o_ref, state_ref):
    tt = su_ref.shape[1]
    nchunk = tt // GDN_CHUNK
    hd = GDN_HEAD_DIM
    heads = range(GDN_HEADS)
    dot = functools.partial(jnp.dot, preferred_element_type=F32)
    states = [state_ref[h] for h in heads]
    for c in range(nchunk):
        r0, r1 = c * GDN_CHUNK, (c + 1) * GDN_CHUNK
        for h in heads:
            ws = dot(swq_ref[h, c], states[h].astype(BF16))
            v_new = (su_ref[h, r0:r1, :] - ws[0:GDN_CHUNK]).astype(BF16)
            o = ws[GDN_CHUNK:2 * GDN_CHUNK] + dot(sqk_ref[h, c], v_new)
            gl = segc_ref[r1 - 1:r1, GDN_HEADS + h:GDN_HEADS + h + 1]
            states[h] = states[h] * gl + lax.dot_general(skd_ref[h, r0:r1, :], v_new, (((0,), (0,)), ((), ())),
                                                         preferred_element_type=F32)
            o = o * lax.rsqrt(jnp.mean(o * o, axis=-1, keepdims=True) + NORM_EPS) * nw_ref[...]
            zz = z_ref[r0:r1, h * hd:(h + 1) * hd].astype(F32)
            o_ref[r0:r1, h * hd:(h + 1) * hd] = (o * _silu(zz)).astype(BF16)
        if c == nchunk - 1:
            for h in heads:
                state_ref[h] = states[h]
        yield


def _mixer_kernel(sink_ref, q_ref, kvc_ref, kvp_ref, bias_ref, g_ref, ba_ref, gp_ref, z_ref, nw_ref,
                  attn_ref, gdn_ref, kv_ref, state_ref, su_ref, swq_ref, sqk_ref, skd_ref, segc_ref, *, tiles_per_seq, layer):
    i = pl.program_id(0)
    ntiles = pl.num_programs(0) - 1
    cur = jnp.minimum(i, ntiles - 1)
    prev = jnp.maximum(i - 1, 0)

    @pl.when(i == 0)
    def _():
        su_ref[...] = jnp.zeros_like(su_ref)
        swq_ref[...] = jnp.zeros_like(swq_ref)
        sqk_ref[...] = jnp.zeros_like(sqk_ref)
        skd_ref[...] = jnp.zeros_like(skd_ref)
        segc_ref[...] = jnp.zeros_like(segc_ref)

    @pl.when(prev % tiles_per_seq == 0)
    def _():
        state_ref[...] = jnp.zeros_like(state_ref)

    staged = []
    rec = _gdn_recurrence_steps(su_ref, swq_ref, sqk_ref, skd_ref, segc_ref, z_ref, nw_ref, gdn_ref, state_ref)
    loc = _gdn_chunk_local_steps(g_ref, ba_ref, gp_ref, staged)
    att = _attn_steps(sink_ref, layer, q_ref, kvc_ref, kvp_ref, bias_ref, attn_ref, kv_ref, cur % tiles_per_seq == 0)

    def advance(gen, n):
        for _ in range(n):
            next(gen, None)

    nrec = su_ref.shape[1] // GDN_CHUNK
    nloc = GDN_LOCAL_STEPS
    natt = 1 + (q_ref.shape[0] // WINDOW) * ATTN_KV_HEADS * (ATTN_GROUP // 2)
    for c in range(nrec):
        advance(rec, 1)
        advance(loc, -(-nloc // nrec))
        advance(att, -(-natt // nrec))
    advance(loc, nloc)
    advance(att, natt)

    for h in range(GDN_HEADS):
        u, wq, qk_d, kd = staged[h]
        su_ref[h] = u
        skd_ref[h] = kd
        for c in range(nrec):
            swq_ref[h, c] = wq[c]
            sqk_ref[h, c] = qk_d[c]
    segc_ref[...] = staged[GDN_HEADS]


def _gate_params(a_log, dt_bias):
    gp = jnp.zeros((a_log.shape[0], SUBLANES, GATE_DIM), F32)
    gp = gp.at[:, 0, GDN_HEADS:2 * GDN_HEADS].set(a_log.astype(F32))
    return gp.at[:, 1, GDN_HEADS:2 * GDN_HEADS].set(dt_bias.astype(F32))


def _mixers(qkv, sinks, bias, g, z, ba, gate_params, norm_w, layer, batch, seq):
    m = g.shape[0]
    tt = min(MIX_TT, seq)
    nt = seq // tt
    ntiles = batch * nt
    sub = tt // WINDOW
    nchunk = tt // GDN_CHUNK
    kv_col = ATTN_Q_DIM // (2 * ATTN_KV_DIM)
    cur = lambda i: (jnp.minimum(i, ntiles - 1), 0)
    prev = lambda i: (jnp.maximum(i - 1, 0), 0)
    hd = GDN_HEAD_DIM
    return pl.pallas_call(
        functools.partial(_mixer_kernel, tiles_per_seq=nt, layer=layer),
        grid=(ntiles + 1,),
        in_specs=[pl.BlockSpec(memory_space=pltpu.SMEM),
                  pl.BlockSpec((tt, ATTN_Q_DIM), cur),
                  pl.BlockSpec((tt, 2 * ATTN_KV_DIM), lambda i: (jnp.minimum(i, ntiles - 1), kv_col)),
                  pl.BlockSpec((WINDOW, 2 * ATTN_KV_DIM),
                               lambda i: (jnp.maximum(jnp.minimum(i, ntiles - 1) * sub - 1, 0), kv_col)),
                  pl.BlockSpec((2, ATTN_HEADS, 2 * WINDOW, WINDOW), lambda i: (0, 0, 0, 0)),
                  pl.BlockSpec((tt, GDN_QKV_DIM), cur),
                  pl.BlockSpec((tt, GATE_DIM), cur),
                  pl.BlockSpec((None, SUBLANES, GATE_DIM), lambda i: (layer, 0, 0)),
                  pl.BlockSpec((tt, GDN_DIM), prev),
                  pl.BlockSpec((None, 1, GDN_HEAD_DIM), lambda i: (layer, 0, 0))],
        out_specs=[pl.BlockSpec((tt, ATTN_Q_DIM), cur),
                   pl.BlockSpec((tt, GDN_DIM), prev)],
        out_shape=[jax.ShapeDtypeStruct((m, ATTN_Q_DIM), BF16),
                   jax.ShapeDtypeStruct((m, GDN_DIM), BF16)],
        scratch_shapes=[pltpu.VMEM((tt + WINDOW, 2 * ATTN_KV_DIM), BF16),
                        pltpu.VMEM((GDN_HEADS, hd, hd), F32),
                        pltpu.VMEM((GDN_HEADS, tt, hd), F32),
                        pltpu.VMEM((GDN_HEADS, nchunk, 2 * GDN_CHUNK, hd), BF16),
                        pltpu.VMEM((GDN_HEADS, nchunk, GDN_CHUNK, GDN_CHUNK), BF16),
                        pltpu.VMEM((GDN_HEADS, tt, hd), BF16),
                        pltpu.VMEM((tt, GATE_DIM), F32)],
        compiler_params=pltpu.CompilerParams(dimension_semantics=("arbitrary",),
                                             vmem_limit_bytes=VMEM_LIMIT),
        name="mixers",
    )(sinks, qkv, qkv, qkv, bias, g, ba, gate_params, z, norm_w)


def _mix_ffn_kernel(x_ref, attn_ref, gdn_ref, wout_ref, nw_ref, win_ref, cw_ref, cb_ref, wd_ref, fnw_ref,
                    o_ref, gbuf_ref, carry_ref, act_ref, *, final):
    tm = x_ref.shape[0]
    halo = SUBLANES

    @pl.when(pl.program_id(1) == 0)
    def _():
        carry_ref[...] = jnp.zeros_like(carry_ref)

    def rms(x, w):
        return x * lax.rsqrt(jnp.mean(x * x, axis=-1, keepdims=True) + NORM_EPS) * w

    dot = functools.partial(jnp.dot, preferred_element_type=F32)
    x1 = (x_ref[...] + dot(attn_ref[...], wout_ref[0:ATTN_Q_DIM, :])
          + dot(gdn_ref[...], wout_ref[ATTN_Q_DIM:MIX_DIM, :]))
    h = rms(x1, nw_ref[0:1, :]).astype(BF16)
    for c in range(D_FF // FFN_FC):
        c0, c1 = c * FFN_FC, (c + 1) * FFN_FC
        gate = dot(h, win_ref[:, c0:c1])
        up = dot(h, win_ref[:, D_FF + c0:D_FF + c1])
        gbuf_ref[0:halo, :] = carry_ref[:, c0:c1]
        gbuf_ref[halo:halo + tm, :] = gate
        carry_ref[:, c0:c1] = gate[tm - halo:tm, :]
        conv = gate * cw_ref[2, 0:1, c0:c1] + cb_ref[0:1, c0:c1]
        conv = conv + gbuf_ref[halo - 1:halo - 1 + tm, :] * cw_ref[1, 0:1, c0:c1]
        conv = conv + gbuf_ref[halo - 2:halo - 2 + tm, :] * cw_ref[0, 0:1, c0:c1]
        act_ref[:, c0:c1] = (_silu(conv) * up).astype(BF16)
    out = x1 + dot(act_ref[...], wd_ref[...])
    if final:
        out = rms(out, fnw_ref[0:1, :])
    o_ref[...] = out


def _mix_ffn(x, attn, gdn, w_out, norm_w8, w_ffn_in, conv_w8, conv_b8, w_down, final_norm8, layer, batch, seq, final):
    m = x.shape[0]
    tm = min(FFN_TM, seq)
    nt = seq // tm
    row = lambda b, t: (b * nt + t, 0)
    at_layer = lambda b, t: (layer, 0, 0)
    resident = lambda shape: pl.BlockSpec((None,) + shape, at_layer, pipeline_mode=pl.Buffered(1))
    return pl.pallas_call(
        functools.partial(_mix_ffn_kernel, final=final),
        grid=(batch, nt),
        in_specs=[pl.BlockSpec((tm, D_MODEL), row),
                  pl.BlockSpec((tm, ATTN_Q_DIM), row),
                  pl.BlockSpec((tm, GDN_DIM), row),
                  resident((MIX_DIM, D_MODEL)),
                  pl.BlockSpec((None, SUBLANES, D_MODEL), at_layer),
                  resident((D_MODEL, 2 * D_FF)),
                  pl.BlockSpec((None, FFN_CONV, SUBLANES, D_FF), lambda b, t: (layer, 0, 0, 0)),
                  pl.BlockSpec((None, SUBLANES, D_FF), at_layer),
                  resident((D_FF, D_MODEL)),
                  pl.BlockSpec((SUBLANES, D_MODEL), lambda b, t: (0, 0))],
        out_specs=pl.BlockSpec((tm, D_MODEL), row),
        out_shape=jax.ShapeDtypeStruct((m, D_MODEL), F32),
        scratch_shapes=[pltpu.VMEM((tm + SUBLANES, FFN_FC), F32),
                        pltpu.VMEM((SUBLANES, D_FF), F32),
                        pltpu.VMEM((tm, D_FF), BF16)],
        compiler_params=pltpu.CompilerParams(dimension_semantics=("parallel", "arbitrary"),
                                             vmem_limit_bytes=VMEM_LIMIT),
        name="mix_ffn",
    )(x, attn, gdn, w_out, norm_w8, w_ffn_in, conv_w8, conv_b8, w_down, final_norm8)


def kernel(x, attn_norm, w_in, attn_sinks, gdn_conv_w, gdn_a_log, gdn_dt_bias, gdn_norm, w_out, ffn_norm,
           w_ffn_in, ffn_conv_w, ffn_conv_b, w_down, final_norm):
    batch, seq, _ = x.shape
    assert seq % WINDOW == 0 and seq % GDN_CHUNK == 0
    xf = x.reshape(batch * seq, D_MODEL).astype(F32)
    bias = _attn_bias()
    depth = w_in.shape[0]
    w_in_b = jnp.pad(w_in, ((0, 0), (0, 0), (0, IN_PAD - IN_DIM))).astype(BF16)
    w_out_b, w_ffn_in_b, w_down_b = w_out.astype(BF16), w_ffn_in.astype(BF16), w_down.astype(BF16)
    attn_norm8, gdn_conv8 = _rows8(attn_norm), _rows8(gdn_conv_w)
    ffn_norm8, ffn_conv8, ffn_bias8, final_norm8 = _rows8(ffn_norm), _rows8(ffn_conv_w), _rows8(ffn_conv_b), _rows8(final_norm)
    gate_params = _gate_params(gdn_a_log, gdn_dt_bias)
    sinks = attn_sinks.astype(F32)
    gdn_norm3 = gdn_norm.astype(F32)[:, None, :]
    for l in range(depth):
        qkv, g, z, ba = _inproj(xf, attn_norm8, w_in_b, gdn_conv8, l, batch, seq)
        attn, gdn = _mixers(qkv, sinks, bias, g, z, ba, gate_params, gdn_norm3, l, batch, seq)
        xf = _mix_ffn(xf, attn, gdn, w_out_b, ffn_norm8, w_ffn_in_b, ffn_conv8, ffn_bias8, w_down_b, final_norm8,
                      l, batch, seq, final=(l == depth - 1))
    return xf.reshape(batch, seq, D_MODEL).astype(x.dtype)
```

```python
import functools

import numpy as np
import jax
import jax.numpy as jnp
from jax import lax
from jax.experimental import pallas as pl
from jax.experimental.pallas import tpu as pltpu

D_MODEL = 1024
ATTN_HEADS = 8
ATTN_KV_HEADS = 2
ATTN_GROUP = ATTN_HEADS // ATTN_KV_HEADS
ATTN_HEAD_DIM = 64
WINDOW = 128
GDN_HEADS = 4
GDN_HEAD_DIM = 128
GDN_CHUNK = 64
GDN_CONV = 4
D_FF = 2816
FFN_CONV = 3
NORM_EPS = 1e-6

ATTN_Q_DIM = ATTN_HEADS * ATTN_HEAD_DIM
ATTN_KV_DIM = ATTN_KV_HEADS * ATTN_HEAD_DIM
ATTN_QKV_DIM = ATTN_Q_DIM + 2 * ATTN_KV_DIM
GDN_DIM = GDN_HEADS * GDN_HEAD_DIM
GDN_QKV_DIM = 3 * GDN_DIM
MIX_DIM = ATTN_Q_DIM + GDN_DIM
IN_DIM = ATTN_QKV_DIM + GDN_QKV_DIM + GDN_DIM + 2 * GDN_HEADS
LANES = 128
SUBLANES = 8
GATE_DIM = LANES
IN_PAD = IN_DIM - 2 * GDN_HEADS + GATE_DIM

IN_TM = 1024
MIX_TT = 256
FFN_TM = 1024
FFN_FC = 256
VMEM_LIMIT = 60 * 1024 * 1024

NEG_BIG = -1e30
BF16 = jnp.bfloat16
F32 = jnp.float32


def _mm_nt(a, b):
    return lax.dot_general(a.astype(BF16), b.astype(BF16), (((1,), (1,)), ((), ())),
                           preferred_element_type=F32)


def _split3(a):
    a1 = a.astype(BF16)
    r1 = a - a1.astype(F32)
    a2 = r1.astype(BF16)
    a3 = (r1 - a2.astype(F32)).astype(BF16)
    return a1, a2, a3


def _mul_rows(x, w8):
    n, d = x.shape
    return (x.reshape(n // SUBLANES, SUBLANES, d) * w8[None]).reshape(n, d)


def _rows8(w):
    w = w.astype(F32)
    return jnp.broadcast_to(w[..., None, :], w.shape[:-1] + (SUBLANES, w.shape[-1]))


def _rmsnorm(x, w8):
    return _mul_rows(x * lax.rsqrt(jnp.mean(x * x, axis=-1, keepdims=True) + NORM_EPS), w8)


def _silu(x):
    return x / (1.0 + jnp.exp(-x))


def _inproj_kernel(x_ref, nw_ref, w_ref, cw_ref, qkv_ref, g_ref, z_ref, ba_ref, gbuf_ref):
    tm = x_ref.shape[0]
    hd = GDN_HEAD_DIM
    halo = SUBLANES
    o0, o1, o2, o3 = 0, ATTN_QKV_DIM, ATTN_QKV_DIM + GDN_QKV_DIM, IN_PAD - GATE_DIM
    dot = functools.partial(jnp.dot, preferred_element_type=F32)

    nblk = GDN_QKV_DIM // hd

    @pl.when(pl.program_id(1) == 0)
    def _():
        gbuf_ref[:, 0:halo, :] = jnp.zeros((nblk, halo, hd), F32)

    h = _rmsnorm(x_ref[...], nw_ref[...]).astype(BF16)
    graw = dot(h, w_ref[:, o1:o2])
    for blk in range(nblk):
        gbuf_ref[blk, halo:halo + tm, :] = graw[:, blk * hd:(blk + 1) * hd]
    qkv_ref[...] = dot(h, w_ref[:, o0:o1]).astype(BF16)
    z_ref[...] = dot(h, w_ref[:, o2:o3]).astype(BF16)
    ba_ref[...] = dot(h, w_ref[:, o3:IN_PAD])

    for blk in range(nblk):
        c0, c1 = blk * hd, (blk + 1) * hd
        acc = None
        for j in range(GDN_CONV):
            sh = GDN_CONV - 1 - j
            term = _mul_rows(gbuf_ref[blk, halo - sh:halo - sh + tm, :], cw_ref[j, :, c0:c1])
            acc = term if acc is None else acc + term
        y = _silu(acc)
        part = blk // GDN_HEADS
        if part < 2:
            y = y * lax.rsqrt(jnp.sum(y * y, axis=-1, keepdims=True) + NORM_EPS)
        if part == 0:
            y = y * (hd ** -0.5)
        g_ref[:, c0:c1] = y.astype(BF16)
    gbuf_ref[:, 0:halo, :] = gbuf_ref[:, tm:tm + halo, :]


def _inproj(x, norm_w8, w_pad, conv_w8, layer, batch, seq):
    m = x.shape[0]
    tm = min(IN_TM, seq)
    nt = seq // tm
    row = lambda b, t: (b * nt + t, 0)
    return pl.pallas_call(
        _inproj_kernel,
        grid=(batch, nt),
        in_specs=[pl.BlockSpec((tm, D_MODEL), row),
                  pl.BlockSpec((None, SUBLANES, D_MODEL), lambda b, t: (layer, 0, 0)),
                  pl.BlockSpec((None, D_MODEL, IN_PAD), lambda b, t: (layer, 0, 0), pipeline_mode=pl.Buffered(1)),
                  pl.BlockSpec((None, GDN_CONV, SUBLANES, GDN_QKV_DIM), lambda b, t: (layer, 0, 0, 0))],
        out_specs=[pl.BlockSpec((tm, ATTN_QKV_DIM), row),
                   pl.BlockSpec((tm, GDN_QKV_DIM), row),
                   pl.BlockSpec((tm, GDN_DIM), row),
                   pl.BlockSpec((tm, GATE_DIM), row)],
        out_shape=[jax.ShapeDtypeStruct((m, ATTN_QKV_DIM), BF16),
                   jax.ShapeDtypeStruct((m, GDN_QKV_DIM), BF16),
                   jax.ShapeDtypeStruct((m, GDN_DIM), BF16),
                   jax.ShapeDtypeStruct((m, GATE_DIM), F32)],
        scratch_shapes=[pltpu.VMEM((GDN_QKV_DIM // GDN_HEAD_DIM, tm + SUBLANES, GDN_HEAD_DIM), F32)],
        compiler_params=pltpu.CompilerParams(dimension_semantics=("parallel", "arbitrary"),
                                             vmem_limit_bytes=VMEM_LIMIT),
        name="inproj",
    )(x, norm_w8, w_pad, conv_w8)


def _attn_bias():
    qpos = np.arange(WINDOW)[None, :] + WINDOW
    kpos = np.arange(2 * WINDOW)[:, None]
    rel = qpos - kpos
    band = (rel >= 0) & (rel < WINDOW)
    slopes = 2.0 ** (-8.0 * np.arange(1, ATTN_HEADS + 1) / ATTN_HEADS)
    alibi = -slopes[:, None, None] * rel[None].astype(np.float64)
    rest = np.where(band[None], alibi, NEG_BIG)
    first = np.where((band & (kpos >= WINDOW))[None], alibi, NEG_BIG)
    return jnp.asarray(np.stack([first, rest]), dtype=F32)


def _attn_steps(sink_ref, layer, q_ref, kvc_ref, kvp_ref, bias_ref, o_ref, kv_ref, first_tile):
    dh = ATTN_HEAD_DIM
    w = WINDOW
    nsub = q_ref.shape[0] // w
    nkv = kv_ref.shape[0]
    kv_ref[0:w, :] = kvp_ref[...]
    kv_ref[w:nkv, :] = kvc_ref[...]
    ones = jnp.ones((nkv, dh), F32)
    vt_ext = []
    for h in range(ATTN_KV_HEADS):
        vo = ATTN_KV_DIM + h * dh
        v_ext = jnp.concatenate([kv_ref[:, vo:vo + dh].astype(F32), ones], axis=1)
        vt_ext.append(v_ext.T.astype(BF16))
    yield
    for j in range(nsub):
        r0 = j * w
        bias_idx = jnp.where(first_tile, 0, 1) if j == 0 else 1
        for h in range(ATTN_KV_HEADS):
            k = kv_ref[r0:r0 + 2 * w, h * dh:(h + 1) * dh]
            vt = vt_ext[h][:, r0:r0 + 2 * w]
            for gp in range(ATTN_GROUP // 2):
                hq0 = h * ATTN_GROUP + 2 * gp
                res = []
                for hq in (hq0, hq0 + 1):
                    q = q_ref[r0:r0 + w, hq * dh:(hq + 1) * dh] * (dh ** -0.5)
                    s = _mm_nt(k, q) + bias_ref[bias_idx, hq]
                    sink = sink_ref[layer, hq]
                    m = jnp.maximum(jnp.max(s, axis=0, keepdims=True), sink)
                    p = jnp.exp(s - m).astype(BF16)
                    acc = jnp.dot(vt, p, preferred_element_type=F32)
                    den = acc[dh:dh + 1, :] + jnp.exp(sink - m)
                    res.append(acc[0:dh, :] / den)
                o_ref[r0:r0 + w, hq0 * dh:(hq0 + 2) * dh] = jnp.concatenate(res, axis=0).T.astype(BF16)
                yield


GDN_LEVELS = GDN_CHUNK.bit_length() - 1


def _gdn_chunk_local_steps(g_ref, ba_ref, gp_ref, staged):
    tt = g_ref.shape[0]
    nchunk = tt // GDN_CHUNK
    hd = GDN_HEAD_DIM
    heads = range(GDN_HEADS)
    dot = functools.partial(jnp.dot, preferred_element_type=F32)

    ba = ba_ref[...]
    beta_all = 1.0 / (1.0 + jnp.exp(-ba))
    pre = ba + gp_ref[1:2, :]
    softplus = jnp.maximum(pre, 0.0) + jnp.log(1.0 + jnp.exp(-jnp.abs(pre)))
    g_all = -jnp.exp(gp_ref[0:1, :]) * softplus

    ri = lax.broadcasted_iota(jnp.int32, (tt, tt), 0)
    ci = lax.broadcasted_iota(jnp.int32, (tt, tt), 1)
    causal = ((ri // GDN_CHUNK) == (ci // GDN_CHUNK)) & (ri >= ci)

    ltri = jnp.where(causal, 1.0, 0.0).astype(BF16)
    g1, g2, g3 = _split3(g_all)
    gc_all = (dot(ltri, g3) + dot(ltri, g2)) + dot(ltri, g1)
    gc_rows = gc_all.T
    gcl_all = jnp.concatenate(
        [jnp.broadcast_to(gc_all[(c + 1) * GDN_CHUNK - 1:(c + 1) * GDN_CHUNK, :], (GDN_CHUNK, GATE_DIM))
         for c in range(nchunk)], axis=0)
    egc_all = jnp.exp(gc_all)
    ekd_all = jnp.exp(gcl_all - gc_all)

    rc = lax.broadcasted_iota(jnp.int32, (GDN_CHUNK, GDN_CHUNK), 0)
    cc = lax.broadcasted_iota(jnp.int32, (GDN_CHUNK, GDN_CHUNK), 1)
    level_masks = []
    for lv in range(GDN_LEVELS):
        s = 1 << lv
        lower_left = ((rc // (2 * s)) == (cc // (2 * s))) & ((rc // s) > (cc // s))
        level_masks.append(jnp.where(lower_left, 1.0, 0.0).astype(BF16))
    eye = jnp.where(rc == cc, 1.0, 0.0)
    band = jnp.where(rc >= cc, 0.0, NEG_BIG)
    yield

    chunks = [slice(c * GDN_CHUNK, (c + 1) * GDN_CHUNK) for c in range(nchunk)]
    a_lv, qk_d, qd, kd, rhs = [], [], [], [], []
    for h in heads:
        q16 = g_ref[:, h * hd:(h + 1) * hd]
        k16 = g_ref[:, GDN_DIM + h * hd:GDN_DIM + (h + 1) * hd]
        q, k = q16.astype(F32), k16.astype(F32)
        v = g_ref[:, 2 * GDN_DIM + h * hd:2 * GDN_DIM + (h + 1) * hd].astype(F32)
        beta = beta_all[:, h:h + 1]
        gc = gc_all[:, GDN_HEADS + h:GDN_HEADS + h + 1]
        gc_row = gc_rows[GDN_HEADS + h:GDN_HEADS + h + 1, :]
        egc = egc_all[:, GDN_HEADS + h:GDN_HEADS + h + 1]
        kb = k * beta
        kb16 = kb.astype(BF16)
        a_h, qk_h = [], []
        for rs in chunks:
            decay = jnp.exp((gc[rs] - gc_row[:, rs]) + band)
            a_mat = (_mm_nt(kb16[rs], k16[rs]) * decay).astype(BF16)
            qk_h.append((_mm_nt(q16[rs], k16[rs]) * decay).astype(BF16))
            a_h.append([a_mat * m for m in level_masks])
        a_lv.append(a_h)
        qk_d.append(qk_h)
        qd.append((q * egc).astype(BF16))
        kd.append((k * ekd_all[:, GDN_HEADS + h:GDN_HEADS + h + 1]).astype(BF16))
        rhs.append(jnp.concatenate([v * beta, kb * egc], axis=1).astype(BF16))
        yield

    tinv = [[eye - a_lv[h][c][0].astype(F32) for c in range(nchunk)] for h in heads]
    for lv in range(1, GDN_LEVELS):
        for h in heads:
            for c in range(nchunk):
                t16 = tinv[h][c].astype(BF16)
                x = dot(a_lv[h][c][lv], t16).astype(BF16)
                tinv[h][c] = tinv[h][c] - dot(t16, x)
        yield
    for h in heads:
        sols = [dot(tinv[h][c].astype(BF16), rhs[h][chunks[c]]) for c in range(nchunk)]
        wq = [jnp.concatenate([sols[c][:, hd:2 * hd].astype(BF16), qd[h][chunks[c]]], axis=0) for c in range(nchunk)]
        staged.append((jnp.concatenate([sol[:, 0:hd] for sol in sols], axis=0), wq, qk_d[h], kd[h]))
    staged.append(egc_all)
    yield


GDN_LOCAL_STEPS = 1 + GDN_HEADS + (GDN_LEVELS - 1) + 1


def _gdn_recurrence_steps(su_ref, swq_ref, sqk_ref, skd_ref, segc_ref, z_ref, nw_ref, o_ref, state_ref):
    tt = su_ref.shape[1]
    nchunk = tt // GDN_CHUNK
    hd = GDN_HEAD_DIM
    heads = range(GDN_HEADS)
    dot = functools.partial(jnp.dot, preferred_element_type=F32)
    states = [state_ref[h] for h in heads]
    for c in range(nchunk):
        r0, r1 = c * GDN_CHUNK, (c + 1) * GDN_CHUNK
        for h in heads:
            ws = dot(swq_ref[h, c], states[h].astype(BF16))
            v_new = (su_ref[h, r0:r1, :] - ws[0:GDN_CHUNK]).astype(BF16)
            o = ws[GDN_CHUNK:2 * GDN_CHUNK] + dot(sqk_ref[h, c], v_new)
            gl = segc_ref[r1 - 1:r1, GDN_HEADS + h:GDN_HEADS + h + 1]
            states[h] = states[h] * gl + lax.dot_general(skd_ref[h, r0:r1, :], v_new, (((0,), (0,)), ((), ())),
                                                         preferred_element_type=F32)
            o = o * lax.rsqrt(jnp.mean(o * o, axis=-1, keepdims=True) + NORM_EPS) * nw_ref[...]
            zz = z_ref[r0:r1, h * hd:(h + 1) * hd].astype(F32)
            o_ref[r0:r1, h * hd:(h + 1) * hd] = (o * _silu(zz)).astype(BF16)
        if c == nchunk - 1:
            for h in heads:
                state_ref[h] = states[h]
        yield


def _mixer_kernel(sink_ref, q_ref, kvc_ref, kvp_ref, bias_ref, g_ref, ba_ref, gp_ref, z_ref, nw_ref,
                  attn_ref, gdn_ref, kv_ref, state_ref, su_ref, swq_ref, sqk_ref, skd_ref, segc_ref, *, tiles_per_seq, layer):
    i = pl.program_id(0)
    ntiles = pl.num_programs(0) - 1
    cur = jnp.minimum(i, ntiles - 1)
    prev = jnp.maximum(i - 1, 0)

    @pl.when(i == 0)
    def _():
        su_ref[...] = jnp.zeros_like(su_ref)
        swq_ref[...] = jnp.zeros_like(swq_ref)
        sqk_ref[...] = jnp.zeros_like(sqk_ref)
        skd_ref[...] = jnp.zeros_like(skd_ref)
        segc_ref[...] = jnp.zeros_like(segc_ref)

    @pl.when(prev % tiles_per_seq == 0)
    def _():
        state_ref[...] = jnp.zeros_like(state_ref)

    staged = []
    rec = _gdn_recurrence_steps(su_ref, swq_ref, sqk_ref, skd_ref, segc_ref, z_ref, nw_ref, gdn_ref, state_ref)
    loc = _gdn_chunk_local_steps(g_ref, ba_ref, gp_ref, staged)
    att = _attn_steps(sink_ref, layer, q_ref, kvc_ref, kvp_ref, bias_ref, attn_ref, kv_ref, cur % tiles_per_seq == 0)

    def advance(gen, n):
        for _ in range(n):
            next(gen, None)

    nrec = su_ref.shape[1] // GDN_CHUNK
    nloc = GDN_LOCAL_STEPS
    natt = 1 + (q_ref.shape[0] // WINDOW) * ATTN_KV_HEADS * (ATTN_GROUP // 2)
    for c in range(nrec):
        advance(rec, 1)
        advance(loc, -(-nloc // nrec))
        advance(att, -(-natt // nrec))
    advance(loc, nloc)
    advance(att, natt)

    for h in range(GDN_HEADS):
        u, wq, qk_d, kd = staged[h]
        su_ref[h] = u
        skd_ref[h] = kd
        for c in range(nrec):
            swq_ref[h, c] = wq[c]
            sqk_ref[h, c] = qk_d[c]
    segc_ref[...] = staged[GDN_HEADS]


def _gate_params(a_log, dt_bias):
    gp = jnp.zeros((a_log.shape[0], SUBLANES, GATE_DIM), F32)
    gp = gp.at[:, 0, GDN_HEADS:2 * GDN_HEADS].set(a_log.astype(F32))
    return gp.at[:, 1, GDN_HEADS:2 * GDN_HEADS].set(dt_bias.astype(F32))


def _mixers(qkv, sinks, bias, g, z, ba, gate_params, norm_w, layer, batch, seq):
    m = g.shape[0]
    tt = min(MIX_TT, seq)
    nt = seq // tt
    ntiles = batch * nt
    sub = tt // WINDOW
    nchunk = tt // GDN_CHUNK
    kv_col = ATTN_Q_DIM // (2 * ATTN_KV_DIM)
    cur = lambda i: (jnp.minimum(i, ntiles - 1), 0)
    prev = lambda i: (jnp.maximum(i - 1, 0), 0)
    hd = GDN_HEAD_DIM
    return pl.pallas_call(
        functools.partial(_mixer_kernel, tiles_per_seq=nt, layer=layer),
        grid=(ntiles + 1,),
        in_specs=[pl.BlockSpec(memory_space=pltpu.SMEM),
                  pl.BlockSpec((tt, ATTN_Q_DIM), cur),
                  pl.BlockSpec((tt, 2 * ATTN_KV_DIM), lambda i: (jnp.minimum(i, ntiles - 1), kv_col)),
                  pl.BlockSpec((WINDOW, 2 * ATTN_KV_DIM),
                               lambda i: (jnp.maximum(jnp.minimum(i, ntiles - 1) * sub - 1, 0), kv_col)),
                  pl.BlockSpec((2, ATTN_HEADS, 2 * WINDOW, WINDOW), lambda i: (0, 0, 0, 0)),
                  pl.BlockSpec((tt, GDN_QKV_DIM), cur),
                  pl.BlockSpec((tt, GATE_DIM), cur),
                  pl.BlockSpec((None, SUBLANES, GATE_DIM), lambda i: (layer, 0, 0)),
                  pl.BlockSpec((tt, GDN_DIM), prev),
                  pl.BlockSpec((None, 1, GDN_HEAD_DIM), lambda i: (layer, 0, 0))],
        out_specs=[pl.BlockSpec((tt, ATTN_Q_DIM), cur),
                   pl.BlockSpec((tt, GDN_DIM), prev)],
        out_shape=[jax.ShapeDtypeStruct((m, ATTN_Q_DIM), BF16),
                   jax.ShapeDtypeStruct((m, GDN_DIM), BF16)],
        scratch_shapes=[pltpu.VMEM((tt + WINDOW, 2 * ATTN_KV_DIM), BF16),
                        pltpu.VMEM((GDN_HEADS, hd, hd), F32),
                        pltpu.VMEM((GDN_HEADS, tt, hd), F32),
                        pltpu.VMEM((GDN_HEADS, nchunk, 2 * GDN_CHUNK, hd), BF16),
                        pltpu.VMEM((GDN_HEADS, nchunk, GDN_CHUNK, GDN_CHUNK), BF16),
                        pltpu.VMEM((GDN_HEADS, tt, hd), BF16),
                        pltpu.VMEM((tt, GATE_DIM), F32)],
        compiler_params=pltpu.CompilerParams(dimension_semantics=("arbitrary",),
                                             vmem_limit_bytes=VMEM_LIMIT),
        name="mixers",
    )(sinks, qkv, qkv, qkv, bias, g, ba, gate_params, z, norm_w)


def _mix_ffn_kernel(x_ref, attn_ref, gdn_ref, wout_ref, nw_ref, win_ref, cw_ref, cb_ref, wd_ref, fnw_ref,
                    o_ref, gbuf_ref, carry_ref, act_ref, *, final):
    tm = x_ref.shape[0]
    halo = SUBLANES

    @pl.when(pl.program_id(1) == 0)
    def _():
        carry_ref[...] = jnp.zeros_like(carry_ref)

    def rms(x, w):
        return x * lax.rsqrt(jnp.mean(x * x, axis=-1, keepdims=True) + NORM_EPS) * w

    dot = functools.partial(jnp.dot, preferred_element_type=F32)
    x1 = (x_ref[...] + dot(attn_ref[...], wout_ref[0:ATTN_Q_DIM, :])
          + dot(gdn_ref[...], wout_ref[ATTN_Q_DIM:MIX_DIM, :]))
    h = rms(x1, nw_ref[0:1, :]).astype(BF16)
    for c in range(D_FF // FFN_FC):
        c0, c1 = c * FFN_FC, (c + 1) * FFN_FC
        gate = dot(h, win_ref[:, c0:c1])
        up = dot(h, win_ref[:, D_FF + c0:D_FF + c1])
        gbuf_ref[0:halo, :] = carry_ref[:, c0:c1]
        gbuf_ref[halo:halo + tm, :] = gate
        carry_ref[:, c0:c1] = gate[tm - halo:tm, :]
        conv = gate * cw_ref[2, 0:1, c0:c1] + cb_ref[0:1, c0:c1]
        conv = conv + gbuf_ref[halo - 1:halo - 1 + tm, :] * cw_ref[1, 0:1, c0:c1]
        conv = conv + gbuf_ref[halo - 2:halo - 2 + tm, :] * cw_ref[0, 0:1, c0:c1]
        act_ref[:, c0:c1] = (_silu(conv) * up).astype(BF16)
    out = x1 + dot(act_ref[...], wd_ref[...])
    if final:
        out = rms(out, fnw_ref[0:1, :])
    o_ref[...] = out


def _mix_ffn(x, attn, gdn, w_out, norm_w8, w_ffn_in, conv_w8, conv_b8, w_down, final_norm8, layer, batch, seq, final):
    m = x.shape[0]
    tm = min(FFN_TM, seq)
    nt = seq // tm
    row = lambda b, t: (b * nt + t, 0)
    at_layer = lambda b, t: (layer, 0, 0)
    resident = lambda shape: pl.BlockSpec((None,) + shape, at_layer, pipeline_mode=pl.Buffered(1))
    return pl.pallas_call(
        functools.partial(_mix_ffn_kernel, final=final),
        grid=(batch, nt),
        in_specs=[pl.BlockSpec((tm, D_MODEL), row),
                  pl.BlockSpec((tm, ATTN_Q_DIM), row),
                  pl.BlockSpec((tm, GDN_DIM), row),
                  resident((MIX_DIM, D_MODEL)),
                  pl.BlockSpec((None, SUBLANES, D_MODEL), at_layer),
                  resident((D_MODEL, 2 * D_FF)),
                  pl.BlockSpec((None, FFN_CONV, SUBLANES, D_FF), lambda b, t: (layer, 0, 0, 0)),
                  pl.BlockSpec((None, SUBLANES, D_FF), at_layer),
                  resident((D_FF, D_MODEL)),
                  pl.BlockSpec((SUBLANES, D_MODEL), lambda b, t: (0, 0))],
        out_specs=pl.BlockSpec((tm, D_MODEL), row),
        out_shape=jax.ShapeDtypeStruct((m, D_MODEL), F32),
        scratch_shapes=[pltpu.VMEM((tm + SUBLANES, FFN_FC), F32),
                        pltpu.VMEM((SUBLANES, D_FF), F32),
                        pltpu.VMEM((tm, D_FF), BF16)],
        compiler_params=pltpu.CompilerParams(dimension_semantics=("parallel", "arbitrary"),
                                             vmem_limit_bytes=VMEM_LIMIT),
        name="mix_ffn",
    )(x, attn, gdn, w_out, norm_w8, w_ffn_in, conv_w8, conv_b8, w_down, final_norm8)


def kernel(x, attn_norm, w_in, attn_sinks, gdn_conv_w, gdn_a_log, gdn_dt_bias, gdn_norm, w_out, ffn_norm,
           w_ffn_in, ffn_conv_w, ffn_conv_b, w_down, final_norm):
    batch, seq, _ = x.shape
    assert x.shape[-1] == D_MODEL and seq % WINDOW == 0 and seq % GDN_CHUNK == 0
    for tile in (IN_TM, MIX_TT, FFN_TM):
        assert seq % min(tile, seq) == 0 and min(tile, seq) % WINDOW == 0
    xf = x.reshape(batch * seq, D_MODEL).astype(F32)
    bias = _attn_bias()
    depth = w_in.shape[0]
    w_in_b = jnp.pad(w_in, ((0, 0), (0, 0), (0, IN_PAD - IN_DIM))).astype(BF16)
    w_out_b, w_ffn_in_b, w_down_b = w_out.astype(BF16), w_ffn_in.astype(BF16), w_down.astype(BF16)
    attn_norm8, gdn_conv8 = _rows8(attn_norm), _rows8(gdn_conv_w)
    ffn_norm8, ffn_conv8, ffn_bias8, final_norm8 = _rows8(ffn_norm), _rows8(ffn_conv_w), _rows8(ffn_conv_b), _rows8(final_norm)
    gate_params = _gate_params(gdn_a_log, gdn_dt_bias)
    sinks = attn_sinks.astype(F32)
    gdn_norm3 = gdn_norm.astype(F32)[:, None, :]
    for l in range(depth):
        qkv, g, z, ba = _inproj(xf, attn_norm8, w_in_b, gdn_conv8, l, batch, seq)
        attn, gdn = _mixers(qkv, sinks, bias, g, z, ba, gate_params, gdn_norm3, l, batch, seq)
        xf = _mix_ffn(xf, attn, gdn, w_out_b, ffn_norm8, w_ffn_in_b, ffn_conv8, ffn_bias8, w_down_b, final_norm8,
                      l, batch, seq, final=(l == depth - 1))
    return xf.reshape(batch, seq, D_MODEL).astype(x.dtype)
```

```python
import functools

import numpy as np
import jax
import jax.numpy as jnp
from jax import lax
from jax.experimental import pallas as pl
from jax.experimental.pallas import tpu as pltpu

D_MODEL = 1024
ATTN_HEADS = 8
ATTN_KV_HEADS = 2
ATTN_GROUP = ATTN_HEADS // ATTN_KV_HEADS
ATTN_HEAD_DIM = 64
WINDOW = 128
GDN_HEADS = 4
GDN_HEAD_DIM = 128
GDN_CHUNK = 128
GDN_CONV = 4
D_FF = 2816
FFN_CONV = 3
NORM_EPS = 1e-6

ATTN_Q_DIM = ATTN_HEADS * ATTN_HEAD_DIM
ATTN_KV_DIM = ATTN_KV_HEADS * ATTN_HEAD_DIM
ATTN_QKV_DIM = ATTN_Q_DIM + 2 * ATTN_KV_DIM
GDN_DIM = GDN_HEADS * GDN_HEAD_DIM
GDN_QKV_DIM = 3 * GDN_DIM
MIX_DIM = ATTN_Q_DIM + GDN_DIM
IN_DIM = ATTN_QKV_DIM + GDN_QKV_DIM + GDN_DIM + 2 * GDN_HEADS
LANES = 128
SUBLANES = 8
GATE_DIM = LANES
IN_PAD = IN_DIM - 2 * GDN_HEADS + GATE_DIM

IN_TM = 1024
MIX_TT = 256
FFN_TM = 1024
FFN_FC = 256
VMEM_LIMIT = 60 * 1024 * 1024

NEG_BIG = -1e30
BF16 = jnp.bfloat16
F32 = jnp.float32


def _mm_nt(a, b):
    return lax.dot_general(a.astype(BF16), b.astype(BF16), (((1,), (1,)), ((), ())),
                           preferred_element_type=F32)


def _split3(a):
    a1 = a.astype(BF16)
    r1 = a - a1.astype(F32)
    a2 = r1.astype(BF16)
    a3 = (r1 - a2.astype(F32)).astype(BF16)
    return a1, a2, a3


def _mul_rows(x, w8):
    n, d = x.shape
    return (x.reshape(n // SUBLANES, SUBLANES, d) * w8[None]).reshape(n, d)


def _rows8(w):
    w = w.astype(F32)
    return jnp.broadcast_to(w[..., None, :], w.shape[:-1] + (SUBLANES, w.shape[-1]))


def _rmsnorm(x, w8):
    return _mul_rows(x * lax.rsqrt(jnp.mean(x * x, axis=-1, keepdims=True) + NORM_EPS), w8)


def _silu(x):
    return x / (1.0 + jnp.exp(-x))


def _inproj_kernel(x_ref, nw_ref, w_ref, cw_ref, qkv_ref, g_ref, z_ref, ba_ref, gbuf_ref):
    tm = x_ref.shape[0]
    hd = GDN_HEAD_DIM
    halo = SUBLANES
    o0, o1, o2, o3 = 0, ATTN_QKV_DIM, ATTN_QKV_DIM + GDN_QKV_DIM, IN_PAD - GATE_DIM
    dot = functools.partial(jnp.dot, preferred_element_type=F32)

    nblk = GDN_QKV_DIM // hd

    @pl.when(pl.program_id(1) == 0)
    def _():
        gbuf_ref[:, 0:halo, :] = jnp.zeros((nblk, halo, hd), F32)

    h = _rmsnorm(x_ref[...], nw_ref[...]).astype(BF16)
    graw = dot(h, w_ref[:, o1:o2])
    for blk in range(nblk):
        gbuf_ref[blk, halo:halo + tm, :] = graw[:, blk * hd:(blk + 1) * hd]
    qkv_ref[...] = dot(h, w_ref[:, o0:o1]).astype(BF16)
    z_ref[...] = dot(h, w_ref[:, o2:o3]).astype(BF16)
    ba_ref[...] = dot(h, w_ref[:, o3:IN_PAD])

    for blk in range(nblk):
        c0, c1 = blk * hd, (blk + 1) * hd
        acc = None
        for j in range(GDN_CONV):
            sh = GDN_CONV - 1 - j
            term = _mul_rows(gbuf_ref[blk, halo - sh:halo - sh + tm, :], cw_ref[j, :, c0:c1])
            acc = term if acc is None else acc + term
        y = _silu(acc)
        part = blk // GDN_HEADS
        if part < 2:
            y = y * lax.rsqrt(jnp.sum(y * y, axis=-1, keepdims=True) + NORM_EPS)
        if part == 0:
            y = y * (hd ** -0.5)
        g_ref[:, c0:c1] = y.astype(BF16)
    gbuf_ref[:, 0:halo, :] = gbuf_ref[:, tm:tm + halo, :]


def _inproj(x, norm_w8, w_pad, conv_w8, layer, batch, seq):
    m = x.shape[0]
    tm = min(IN_TM, seq)
    nt = seq // tm
    row = lambda b, t: (b * nt + t, 0)
    return pl.pallas_call(
        _inproj_kernel,
        grid=(batch, nt),
        in_specs=[pl.BlockSpec((tm, D_MODEL), row),
                  pl.BlockSpec((None, SUBLANES, D_MODEL), lambda b, t: (layer, 0, 0)),
                  pl.BlockSpec((None, D_MODEL, IN_PAD), lambda b, t: (layer, 0, 0), pipeline_mode=pl.Buffered(1)),
                  pl.BlockSpec((None, GDN_CONV, SUBLANES, GDN_QKV_DIM), lambda b, t: (layer, 0, 0, 0))],
        out_specs=[pl.BlockSpec((tm, ATTN_QKV_DIM), row),
                   pl.BlockSpec((tm, GDN_QKV_DIM), row),
                   pl.BlockSpec((tm, GDN_DIM), row),
                   pl.BlockSpec((tm, GATE_DIM), row)],
        out_shape=[jax.ShapeDtypeStruct((m, ATTN_QKV_DIM), BF16),
                   jax.ShapeDtypeStruct((m, GDN_QKV_DIM), BF16),
                   jax.ShapeDtypeStruct((m, GDN_DIM), BF16),
                   jax.ShapeDtypeStruct((m, GATE_DIM), F32)],
        scratch_shapes=[pltpu.VMEM((GDN_QKV_DIM // GDN_HEAD_DIM, tm + SUBLANES, GDN_HEAD_DIM), F32)],
        compiler_params=pltpu.CompilerParams(dimension_semantics=("parallel", "arbitrary"),
                                             vmem_limit_bytes=VMEM_LIMIT),
        name="inproj",
    )(x, norm_w8, w_pad, conv_w8)


def _attn_bias():
    qpos = np.arange(WINDOW)[None, :] + WINDOW
    kpos = np.arange(2 * WINDOW)[:, None]
    rel = qpos - kpos
    band = (rel >= 0) & (rel < WINDOW)
    slopes = 2.0 ** (-8.0 * np.arange(1, ATTN_HEADS + 1) / ATTN_HEADS)
    alibi = -slopes[:, None, None] * rel[None].astype(np.float64)
    rest = np.where(band[None], alibi, NEG_BIG)
    first = np.where((band & (kpos >= WINDOW))[None], alibi, NEG_BIG)
    return jnp.asarray(np.stack([first, rest]), dtype=F32)


def _attn_steps(sink_ref, layer, q_ref, kvc_ref, kvp_ref, bias_ref, o_ref, kv_ref, first_tile):
    dh = ATTN_HEAD_DIM
    w = WINDOW
    nsub = q_ref.shape[0] // w
    nkv = kv_ref.shape[0]
    kv_ref[0:w, :] = kvp_ref[...]
    kv_ref[w:nkv, :] = kvc_ref[...]
    ones = jnp.ones((nkv, dh), F32)
    vt_ext = []
    for h in range(ATTN_KV_HEADS):
        vo = ATTN_KV_DIM + h * dh
        v_ext = jnp.concatenate([kv_ref[:, vo:vo + dh].astype(F32), ones], axis=1)
        vt_ext.append(v_ext.T.astype(BF16))
    yield
    for j in range(nsub):
        r0 = j * w
        bias_idx = jnp.where(first_tile, 0, 1) if j == 0 else 1
        for h in range(ATTN_KV_HEADS):
            k = kv_ref[r0:r0 + 2 * w, h * dh:(h + 1) * dh]
            vt = vt_ext[h][:, r0:r0 + 2 * w]
            for gp in range(ATTN_GROUP // 2):
                hq0 = h * ATTN_GROUP + 2 * gp
                res = []
                for hq in (hq0, hq0 + 1):
                    q = q_ref[r0:r0 + w, hq * dh:(hq + 1) * dh] * (dh ** -0.5)
                    s = _mm_nt(k, q) + bias_ref[bias_idx, hq]
                    sink = sink_ref[layer, hq]
                    m = jnp.maximum(jnp.max(s, axis=0, keepdims=True), sink)
                    p = jnp.exp(s - m).astype(BF16)
                    acc = jnp.dot(vt, p, preferred_element_type=F32)
                    den = acc[dh:dh + 1, :] + jnp.exp(sink - m)
                    res.append(acc[0:dh, :] / den)
                o_ref[r0:r0 + w, hq0 * dh:(hq0 + 2) * dh] = jnp.concatenate(res, axis=0).T.astype(BF16)
                yield


GDN_LEVELS = GDN_CHUNK.bit_length() - 1


def _gdn_chunk_local_steps(g_ref, ba_ref, gp_ref, staged):
    tt = g_ref.shape[0]
    nchunk = tt // GDN_CHUNK
    hd = GDN_HEAD_DIM
    heads = range(GDN_HEADS)
    dot = functools.partial(jnp.dot, preferred_element_type=F32)

    ba = ba_ref[...]
    beta_all = 1.0 / (1.0 + jnp.exp(-ba))
    pre = ba + gp_ref[1:2, :]
    softplus = jnp.maximum(pre, 0.0) + jnp.log(1.0 + jnp.exp(-jnp.abs(pre)))
    g_all = -jnp.exp(gp_ref[0:1, :]) * softplus

    ri = lax.broadcasted_iota(jnp.int32, (tt, tt), 0)
    ci = lax.broadcasted_iota(jnp.int32, (tt, tt), 1)
    causal = ((ri // GDN_CHUNK) == (ci // GDN_CHUNK)) & (ri >= ci)

    ltri = jnp.where(causal, 1.0, 0.0).astype(BF16)
    g1, g2, g3 = _split3(g_all)
    gc_all = (dot(ltri, g3) + dot(ltri, g2)) + dot(ltri, g1)
    gc_rows = gc_all.T
    gcl_all = jnp.concatenate(
        [jnp.broadcast_to(gc_all[(c + 1) * GDN_CHUNK - 1:(c + 1) * GDN_CHUNK, :], (GDN_CHUNK, GATE_DIM))
         for c in range(nchunk)], axis=0)
    egc_all = jnp.exp(gc_all)
    ekd_all = jnp.exp(gcl_all - gc_all)

    rc = lax.broadcasted_iota(jnp.int32, (GDN_CHUNK, GDN_CHUNK), 0)
    cc = lax.broadcasted_iota(jnp.int32, (GDN_CHUNK, GDN_CHUNK), 1)
    level_masks = []
    for lv in range(GDN_LEVELS):
        s = 1 << lv
        lower_left = ((rc // (2 * s)) == (cc // (2 * s))) & ((rc // s) > (cc // s))
        level_masks.append(jnp.where(lower_left, 1.0, 0.0).astype(BF16))
    eye = jnp.where(rc == cc, 1.0, 0.0)
    band = jnp.where(rc >= cc, 0.0, NEG_BIG)
    yield

    chunks = [slice(c * GDN_CHUNK, (c + 1) * GDN_CHUNK) for c in range(nchunk)]
    a_lv, qk_d, qd, kd, rhs = [], [], [], [], []
    for h in heads:
        q16 = g_ref[:, h * hd:(h + 1) * hd]
        k16 = g_ref[:, GDN_DIM + h * hd:GDN_DIM + (h + 1) * hd]
        q, k = q16.astype(F32), k16.astype(F32)
        v = g_ref[:, 2 * GDN_DIM + h * hd:2 * GDN_DIM + (h + 1) * hd].astype(F32)
        beta = beta_all[:, h:h + 1]
        gc = gc_all[:, GDN_HEADS + h:GDN_HEADS + h + 1]
        gc_row = gc_rows[GDN_HEADS + h:GDN_HEADS + h + 1, :]
        egc = egc_all[:, GDN_HEADS + h:GDN_HEADS + h + 1]
        kb = k * beta
        kb16 = kb.astype(BF16)
        a_h, qk_h = [], []
        for rs in chunks:
            decay = jnp.exp((gc[rs] - gc_row[:, rs]) + band)
            a_mat = (_mm_nt(kb16[rs], k16[rs]) * decay).astype(BF16)
            qk_h.append((_mm_nt(q16[rs], k16[rs]) * decay).astype(BF16))
            a_h.append([a_mat * m for m in level_masks])
        a_lv.append(a_h)
        qk_d.append(qk_h)
        qd.append((q * egc).astype(BF16))
        kd.append((k * ekd_all[:, GDN_HEADS + h:GDN_HEADS + h + 1]).astype(BF16))
        rhs.append(jnp.concatenate([v * beta, kb * egc], axis=1).astype(BF16))
        yield

    tinv = [[eye - a_lv[h][c][0].astype(F32) for c in range(nchunk)] for h in heads]
    for lv in range(1, GDN_LEVELS):
        for h in heads:
            for c in range(nchunk):
                t16 = tinv[h][c].astype(BF16)
                x = dot(a_lv[h][c][lv], t16).astype(BF16)
                tinv[h][c] = tinv[h][c] - dot(t16, x)
        yield
    for h in heads:
        sols = [dot(tinv[h][c].astype(BF16), rhs[h][chunks[c]]) for c in range(nchunk)]
        wq = [jnp.concatenate([sols[c][:, hd:2 * hd].astype(BF16), qd[h][chunks[c]]], axis=0) for c in range(nchunk)]
        staged.append((jnp.concatenate([sol[:, 0:hd] for sol in sols], axis=0), wq, qk_d[h], kd[h]))
    staged.append(egc_all)
    yield


GDN_LOCAL_STEPS = 1 + GDN_HEADS + (GDN_LEVELS - 1) + 1


def _gdn_recurrence_steps(su_ref, swq_ref, sqk_ref, skd_ref, segc_ref, z_ref, nw_ref, o_ref, state_ref):
    tt = su_ref.shape[1]
    nchunk = tt // GDN_CHUNK
    hd = GDN_HEAD_DIM
    heads = range(GDN_HEADS)
    dot = functools.partial(jnp.dot, preferred_element_type=F32)
    states = [state_ref[h] for h in heads]
    for c in range(nchunk):
        r0, r1 = c * GDN_CHUNK, (c + 1) * GDN_CHUNK
        for h in heads:
            ws = dot(swq_ref[h, c], states[h].astype(BF16))
            v_new = (su_ref[h, r0:r1, :] - ws[0:GDN_CHUNK]).astype(BF16)
            o = ws[GDN_CHUNK:2 * GDN_CHUNK] + dot(sqk_ref[h, c], v_new)
            gl = segc_ref[r1 - 1:r1, GDN_HEADS + h:GDN_HEADS + h + 1]
            states[h] = states[h] * gl + lax.dot_general(skd_ref[h, r0:r1, :], v_new, (((0,), (0,)), ((), ())),
                                                         preferred_element_type=F32)
            o = o * lax.rsqrt(jnp.mean(o * o, axis=-1, keepdims=True) + NORM_EPS) * nw_ref[...]
            zz = z_ref[r0:r1, h * hd:(h + 1) * hd].astype(F32)
            o_ref[r0:r1, h * hd:(h + 1) * hd] = (o * _silu(zz)).astype(BF16)
        if c == nchunk - 1:
            for h in heads:
                state_ref[h] = states[h]
        yield


def _mixer_kernel(sink_ref, q_ref, kvc_ref, kvp_ref, bias_ref, g_ref, ba_ref, gp_ref, z_ref, nw_ref,
                  attn_ref, gdn_ref, kv_ref, state_ref, su_ref, swq_ref, sqk_ref, skd_ref, segc_ref, *, tiles_per_seq, layer):
    i = pl.program_id(0)
    ntiles = pl.num_programs(0) - 1
    cur = jnp.minimum(i, ntiles - 1)
    prev = jnp.maximum(i - 1, 0)

    @pl.when(i == 0)
    def _():
        su_ref[...] = jnp.zeros_like(su_ref)
        swq_ref[...] = jnp.zeros_like(swq_ref)
        sqk_ref[...] = jnp.zeros_like(sqk_ref)
        skd_ref[...] = jnp.zeros_like(skd_ref)
        segc_ref[...] = jnp.zeros_like(segc_ref)

    @pl.when(prev % tiles_per_seq == 0)
    def _():
        state_ref[...] = jnp.zeros_like(state_ref)

    staged = []
    rec = _gdn_recurrence_steps(su_ref, swq_ref, sqk_ref, skd_ref, segc_ref, z_ref, nw_ref, gdn_ref, state_ref)
    loc = _gdn_chunk_local_steps(g_ref, ba_ref, gp_ref, staged)
    att = _attn_steps(sink_ref, layer, q_ref, kvc_ref, kvp_ref, bias_ref, attn_ref, kv_ref, cur % tiles_per_seq == 0)

    def advance(gen, n):
        for _ in range(n):
            next(gen, None)

    nrec = su_ref.shape[1] // GDN_CHUNK
    nloc = GDN_LOCAL_STEPS
    natt = 1 + (q_ref.shape[0] // WINDOW) * ATTN_KV_HEADS * (ATTN_GROUP // 2)
    for c in range(nrec):
        advance(rec, 1)
        advance(loc, -(-nloc // nrec))
        advance(att, -(-natt // nrec))
    advance(loc, nloc)
    advance(att, natt)

    for h in range(GDN_HEADS):
        u, wq, qk_d, kd = staged[h]
        su_ref[h] = u
        skd_ref[h] = kd
        for c in range(nrec):
            swq_ref[h, c] = wq[c]
            sqk_ref[h, c] = qk_d[c]
    segc_ref[...] = staged[GDN_HEADS]


def _gate_params(a_log, dt_bias):
    gp = jnp.zeros((a_log.shape[0], SUBLANES, GATE_DIM), F32)
    gp = gp.at[:, 0, GDN_HEADS:2 * GDN_HEADS].set(a_log.astype(F32))
    return gp.at[:, 1, GDN_HEADS:2 * GDN_HEADS].set(dt_bias.astype(F32))


def _mixers(qkv, sinks, bias, g, z, ba, gate_params, norm_w, layer, batch, seq):
    m = g.shape[0]
    tt = min(MIX_TT, seq)
    nt = seq // tt
    ntiles = batch * nt
    sub = tt // WINDOW
    nchunk = tt // GDN_CHUNK
    kv_col = ATTN_Q_DIM // (2 * ATTN_KV_DIM)
    cur = lambda i: (jnp.minimum(i, ntiles - 1), 0)
    prev = lambda i: (jnp.maximum(i - 1, 0), 0)
    hd = GDN_HEAD_DIM
    return pl.pallas_call(
        functools.partial(_mixer_kernel, tiles_per_seq=nt, layer=layer),
        grid=(ntiles + 1,),
        in_specs=[pl.BlockSpec(memory_space=pltpu.SMEM),
                  pl.BlockSpec((tt, ATTN_Q_DIM), cur),
                  pl.BlockSpec((tt, 2 * ATTN_KV_DIM), lambda i: (jnp.minimum(i, ntiles - 1), kv_col)),
                  pl.BlockSpec((WINDOW, 2 * ATTN_KV_DIM),
                               lambda i: (jnp.maximum(jnp.minimum(i, ntiles - 1) * sub - 1, 0), kv_col)),
                  pl.BlockSpec((2, ATTN_HEADS, 2 * WINDOW, WINDOW), lambda i: (0, 0, 0, 0)),
                  pl.BlockSpec((tt, GDN_QKV_DIM), cur),
                  pl.BlockSpec((tt, GATE_DIM), cur),
                  pl.BlockSpec((None, SUBLANES, GATE_DIM), lambda i: (layer, 0, 0)),
                  pl.BlockSpec((tt, GDN_DIM), prev),
                  pl.BlockSpec((None, 1, GDN_HEAD_DIM), lambda i: (layer, 0, 0))],
        out_specs=[pl.BlockSpec((tt, ATTN_Q_DIM), cur),
                   pl.BlockSpec((tt, GDN_DIM), prev)],
        out_shape=[jax.ShapeDtypeStruct((m, ATTN_Q_DIM), BF16),
                   jax.ShapeDtypeStruct((m, GDN_DIM), BF16)],
        scratch_shapes=[pltpu.VMEM((tt + WINDOW, 2 * ATTN_KV_DIM), BF16),
                        pltpu.VMEM((GDN_HEADS, hd, hd), F32),
                        pltpu.VMEM((GDN_HEADS, tt, hd), F32),
                        pltpu.VMEM((GDN_HEADS, nchunk, 2 * GDN_CHUNK, hd), BF16),
                        pltpu.VMEM((GDN_HEADS, nchunk, GDN_CHUNK, GDN_CHUNK), BF16),
                        pltpu.VMEM((GDN_HEADS, tt, hd), BF16),
                        pltpu.VMEM((tt, GATE_DIM), F32)],
        compiler_params=pltpu.CompilerParams(dimension_semantics=("arbitrary",),
                                             vmem_limit_bytes=VMEM_LIMIT),
        name="mixers",
    )(sinks, qkv, qkv, qkv, bias, g, ba, gate_params, z, norm_w)


def _mix_ffn_kernel(x_ref, attn_ref, gdn_ref, wout_ref, nw_ref, win_ref, cw_ref, cb_ref, wd_ref, fnw_ref,
                    o_ref, gbuf_ref, carry_ref, act_ref, *, final):
    tm = x_ref.shape[0]
    halo = SUBLANES

    @pl.when(pl.program_id(1) == 0)
    def _():
        carry_ref[...] = jnp.zeros_like(carry_ref)

    def rms(x, w):
        return x * lax.rsqrt(jnp.mean(x * x, axis=-1, keepdims=True) + NORM_EPS) * w

    dot = functools.partial(jnp.dot, preferred_element_type=F32)
    x1 = (x_ref[...] + dot(attn_ref[...], wout_ref[0:ATTN_Q_DIM, :])
          + dot(gdn_ref[...], wout_ref[ATTN_Q_DIM:MIX_DIM, :]))
    h = rms(x1, nw_ref[0:1, :]).astype(BF16)
    for c in range(D_FF // FFN_FC):
        c0, c1 = c * FFN_FC, (c + 1) * FFN_FC
        gate = dot(h, win_ref[:, c0:c1])
        up = dot(h, win_ref[:, D_FF + c0:D_FF + c1])
        gbuf_ref[0:halo, :] = carry_ref[:, c0:c1]
        gbuf_ref[halo:halo + tm, :] = gate
        carry_ref[:, c0:c1] = gate[tm - halo:tm, :]
        conv = gate * cw_ref[2, 0:1, c0:c1] + cb_ref[0:1, c0:c1]
        conv = conv + gbuf_ref[halo - 1:halo - 1 + tm, :] * cw_ref[1, 0:1, c0:c1]
        conv = conv + gbuf_ref[halo - 2:halo - 2 + tm, :] * cw_ref[0, 0:1, c0:c1]
        act_ref[:, c0:c1] = (_silu(conv) * up).astype(BF16)
    out = x1 + dot(act_ref[...], wd_ref[...])
    if final:
        out = rms(out, fnw_ref[0:1, :])
    o_ref[...] = out


def _mix_ffn(x, attn, gdn, w_out, norm_w8, w_ffn_in, conv_w8, conv_b8, w_down, final_norm8, layer, batch, seq, final):
    m = x.shape[0]
    tm = min(FFN_TM, seq)
    nt = seq // tm
    row = lambda b, t: (b * nt + t, 0)
    at_layer = lambda b, t: (layer, 0, 0)
    resident = lambda shape: pl.BlockSpec((None,) + shape, at_layer, pipeline_mode=pl.Buffered(1))
    return pl.pallas_call(
        functools.partial(_mix_ffn_kernel, final=final),
        grid=(batch, nt),
        in_specs=[pl.BlockSpec((tm, D_MODEL), row),
                  pl.BlockSpec((tm, ATTN_Q_DIM), row),
                  pl.BlockSpec((tm, GDN_DIM), row),
                  resident((MIX_DIM, D_MODEL)),
                  pl.BlockSpec((None, SUBLANES, D_MODEL), at_layer),
                  resident((D_MODEL, 2 * D_FF)),
                  pl.BlockSpec((None, FFN_CONV, SUBLANES, D_FF), lambda b, t: (layer, 0, 0, 0)),
                  pl.BlockSpec((None, SUBLANES, D_FF), at_layer),
                  resident((D_FF, D_MODEL)),
                  pl.BlockSpec((SUBLANES, D_MODEL), lambda b, t: (0, 0))],
        out_specs=pl.BlockSpec((tm, D_MODEL), row),
        out_shape=jax.ShapeDtypeStruct((m, D_MODEL), F32),
        scratch_shapes=[pltpu.VMEM((tm + SUBLANES, FFN_FC), F32),
                        pltpu.VMEM((SUBLANES, D_FF), F32),
                        pltpu.VMEM((tm, D_FF), BF16)],
        compiler_params=pltpu.CompilerParams(dimension_semantics=("parallel", "arbitrary"),
                                             vmem_limit_bytes=VMEM_LIMIT),
        name="mix_ffn",
    )(x, attn, gdn, w_out, norm_w8, w_ffn_in, conv_w8, conv_b8, w_down, final_norm8)


def kernel(x, attn_norm, w_in, attn_sinks, gdn_conv_w, gdn_a_log, gdn_dt_bias, gdn_norm, w_out, ffn_norm,
           w_ffn_in, ffn_conv_w, ffn_conv_b, w_down, final_norm):
    batch, seq, _ = x.shape
    assert x.shape[-1] == D_MODEL and seq % WINDOW == 0 and seq % GDN_CHUNK == 0
    for tile in (IN_TM, MIX_TT, FFN_TM):
        assert seq % min(tile, seq) == 0 and min(tile, seq) % WINDOW == 0 and min(tile, seq) % GDN_CHUNK == 0
    xf = x.reshape(batch * seq, D_MODEL).astype(F32)
    bias = _attn_bias()
    depth = w_in.shape[0]
    w_in_b = jnp.pad(w_in, ((0, 0), (0, 0), (0, IN_PAD - IN_DIM))).astype(BF16)
    w_out_b, w_ffn_in_b, w_down_b = w_out.astype(BF16), w_ffn_in.astype(BF16), w_down.astype(BF16)
    attn_norm8, gdn_conv8 = _rows8(attn_norm), _rows8(gdn_conv_w)
    ffn_norm8, ffn_conv8, ffn_bias8, final_norm8 = _rows8(ffn_norm), _rows8(ffn_conv_w), _rows8(ffn_conv_b), _rows8(final_norm)
    gate_params = _gate_params(gdn_a_log, gdn_dt_bias)
    sinks = attn_sinks.astype(F32)
    gdn_norm3 = gdn_norm.astype(F32)[:, None, :]
    for l in range(depth):
        qkv, g, z, ba = _inproj(xf, attn_norm8, w_in_b, gdn_conv8, l, batch, seq)
        attn, gdn = _mixers(qkv, sinks, bias, g, z, ba, gate_params, gdn_norm3, l, batch, seq)
        xf = _mix_ffn(xf, attn, gdn, w_out_b, ffn_norm8, w_ffn_in_b, ffn_conv8, ffn_bias8, w_down_b, final_norm8,
                      l, batch, seq, final=(l == depth - 1))
    return xf.reshape(batch, seq, D_MODEL).astype(x.dtype)
```

```python
import functools

import numpy as np
import jax
import jax.numpy as jnp
from jax import lax
from jax.experimental import pallas as pl
from jax.experimental.pallas import tpu as pltpu

D_MODEL = 1024
ATTN_HEADS = 8
ATTN_KV_HEADS = 2
ATTN_GROUP = ATTN_HEADS // ATTN_KV_HEADS
ATTN_HEAD_DIM = 64
WINDOW = 128
GDN_HEADS = 4
GDN_HEAD_DIM = 128
GDN_CHUNK = 128
GDN_CONV = 4
D_FF = 2816
FFN_CONV = 3
NORM_EPS = 1e-6

ATTN_Q_DIM = ATTN_HEADS * ATTN_HEAD_DIM
ATTN_KV_DIM = ATTN_KV_HEADS * ATTN_HEAD_DIM
ATTN_QKV_DIM = ATTN_Q_DIM + 2 * ATTN_KV_DIM
GDN_DIM = GDN_HEADS * GDN_HEAD_DIM
GDN_QKV_DIM = 3 * GDN_DIM
MIX_DIM = ATTN_Q_DIM + GDN_DIM
IN_DIM = ATTN_QKV_DIM + GDN_QKV_DIM + GDN_DIM + 2 * GDN_HEADS
LANES = 128
SUBLANES = 8
GATE_DIM = LANES
IN_PAD = IN_DIM - 2 * GDN_HEADS + GATE_DIM

IN_TM = 1024
MIX_TT = 512
FFN_TM = 1024
FFN_FC = 256
VMEM_LIMIT = 60 * 1024 * 1024

NEG_BIG = -1e30
BF16 = jnp.bfloat16
F32 = jnp.float32


def _mm_nt(a, b):
    return lax.dot_general(a.astype(BF16), b.astype(BF16), (((1,), (1,)), ((), ())),
                           preferred_element_type=F32)


def _split3(a):
    a1 = a.astype(BF16)
    r1 = a - a1.astype(F32)
    a2 = r1.astype(BF16)
    a3 = (r1 - a2.astype(F32)).astype(BF16)
    return a1, a2, a3


def _mul_rows(x, w8):
    n, d = x.shape
    return (x.reshape(n // SUBLANES, SUBLANES, d) * w8[None]).reshape(n, d)


def _rows8(w):
    w = w.astype(F32)
    return jnp.broadcast_to(w[..., None, :], w.shape[:-1] + (SUBLANES, w.shape[-1]))


def _rmsnorm(x, w8):
    return _mul_rows(x * lax.rsqrt(jnp.mean(x * x, axis=-1, keepdims=True) + NORM_EPS), w8)


def _silu(x):
    return x / (1.0 + jnp.exp(-x))


def _inproj_kernel(x_ref, nw_ref, w_ref, cw_ref, qkv_ref, g_ref, z_ref, ba_ref, gbuf_ref):
    tm = x_ref.shape[0]
    hd = GDN_HEAD_DIM
    halo = SUBLANES
    o0, o1, o2, o3 = 0, ATTN_QKV_DIM, ATTN_QKV_DIM + GDN_QKV_DIM, IN_PAD - GATE_DIM
    dot = functools.partial(jnp.dot, preferred_element_type=F32)

    nblk = GDN_QKV_DIM // hd

    @pl.when(pl.program_id(1) == 0)
    def _():
        gbuf_ref[:, 0:halo, :] = jnp.zeros((nblk, halo, hd), F32)

    h = _rmsnorm(x_ref[...], nw_ref[...]).astype(BF16)
    graw = dot(h, w_ref[:, o1:o2])
    for blk in range(nblk):
        gbuf_ref[blk, halo:halo + tm, :] = graw[:, blk * hd:(blk + 1) * hd]
    qkv_ref[...] = dot(h, w_ref[:, o0:o1]).astype(BF16)
    z_ref[...] = dot(h, w_ref[:, o2:o3]).astype(BF16)
    ba_ref[...] = dot(h, w_ref[:, o3:IN_PAD])

    for blk in range(nblk):
        c0, c1 = blk * hd, (blk + 1) * hd
        acc = None
        for j in range(GDN_CONV):
            sh = GDN_CONV - 1 - j
            term = _mul_rows(gbuf_ref[blk, halo - sh:halo - sh + tm, :], cw_ref[j, :, c0:c1])
            acc = term if acc is None else acc + term
        y = _silu(acc)
        part = blk // GDN_HEADS
        if part < 2:
            y = y * lax.rsqrt(jnp.sum(y * y, axis=-1, keepdims=True) + NORM_EPS)
        if part == 0:
            y = y * (hd ** -0.5)
        g_ref[:, c0:c1] = y.astype(BF16)
    gbuf_ref[:, 0:halo, :] = gbuf_ref[:, tm:tm + halo, :]


def _inproj(x, norm_w8, w_pad, conv_w8, layer, batch, seq):
    m = x.shape[0]
    tm = min(IN_TM, seq)
    nt = seq // tm
    row = lambda b, t: (b * nt + t, 0)
    return pl.pallas_call(
        _inproj_kernel,
        grid=(batch, nt),
        in_specs=[pl.BlockSpec((tm, D_MODEL), row),
                  pl.BlockSpec((None, SUBLANES, D_MODEL), lambda b, t: (layer, 0, 0)),
                  pl.BlockSpec((None, D_MODEL, IN_PAD), lambda b, t: (layer, 0, 0), pipeline_mode=pl.Buffered(1)),
                  pl.BlockSpec((None, GDN_CONV, SUBLANES, GDN_QKV_DIM), lambda b, t: (layer, 0, 0, 0))],
        out_specs=[pl.BlockSpec((tm, ATTN_QKV_DIM), row),
                   pl.BlockSpec((tm, GDN_QKV_DIM), row),
                   pl.BlockSpec((tm, GDN_DIM), row),
                   pl.BlockSpec((tm, GATE_DIM), row)],
        out_shape=[jax.ShapeDtypeStruct((m, ATTN_QKV_DIM), BF16),
                   jax.ShapeDtypeStruct((m, GDN_QKV_DIM), BF16),
                   jax.ShapeDtypeStruct((m, GDN_DIM), BF16),
                   jax.ShapeDtypeStruct((m, GATE_DIM), F32)],
        scratch_shapes=[pltpu.VMEM((GDN_QKV_DIM // GDN_HEAD_DIM, tm + SUBLANES, GDN_HEAD_DIM), F32)],
        compiler_params=pltpu.CompilerParams(dimension_semantics=("parallel", "arbitrary"),
                                             vmem_limit_bytes=VMEM_LIMIT),
        name="inproj",
    )(x, norm_w8, w_pad, conv_w8)


def _attn_bias():
    qpos = np.arange(WINDOW)[None, :] + WINDOW
    kpos = np.arange(2 * WINDOW)[:, None]
    rel = qpos - kpos
    band = (rel >= 0) & (rel < WINDOW)
    slopes = 2.0 ** (-8.0 * np.arange(1, ATTN_HEADS + 1) / ATTN_HEADS)
    alibi = -slopes[:, None, None] * rel[None].astype(np.float64)
    rest = np.where(band[None], alibi, NEG_BIG)
    first = np.where((band & (kpos >= WINDOW))[None], alibi, NEG_BIG)
    return jnp.asarray(np.stack([first, rest]), dtype=F32)


def _attn_steps(sink_ref, layer, q_ref, kvc_ref, kvp_ref, bias_ref, o_ref, kv_ref, first_tile):
    dh = ATTN_HEAD_DIM
    w = WINDOW
    nsub = q_ref.shape[0] // w
    nkv = kv_ref.shape[0]
    kv_ref[0:w, :] = kvp_ref[...]
    kv_ref[w:nkv, :] = kvc_ref[...]
    ones = jnp.ones((nkv, dh), F32)
    vt_ext = []
    for h in range(ATTN_KV_HEADS):
        vo = ATTN_KV_DIM + h * dh
        v_ext = jnp.concatenate([kv_ref[:, vo:vo + dh].astype(F32), ones], axis=1)
        vt_ext.append(v_ext.T.astype(BF16))
    yield
    for j in range(nsub):
        r0 = j * w
        bias_idx = jnp.where(first_tile, 0, 1) if j == 0 else 1
        for h in range(ATTN_KV_HEADS):
            k = kv_ref[r0:r0 + 2 * w, h * dh:(h + 1) * dh]
            vt = vt_ext[h][:, r0:r0 + 2 * w]
            for gp in range(ATTN_GROUP // 2):
                hq0 = h * ATTN_GROUP + 2 * gp
                res = []
                for hq in (hq0, hq0 + 1):
                    q = q_ref[r0:r0 + w, hq * dh:(hq + 1) * dh] * (dh ** -0.5)
                    s = _mm_nt(k, q) + bias_ref[bias_idx, hq]
                    sink = sink_ref[layer, hq]
                    m = jnp.maximum(jnp.max(s, axis=0, keepdims=True), sink)
                    p = jnp.exp(s - m).astype(BF16)
                    acc = jnp.dot(vt, p, preferred_element_type=F32)
                    den = acc[dh:dh + 1, :] + jnp.exp(sink - m)
                    res.append(acc[0:dh, :] / den)
                o_ref[r0:r0 + w, hq0 * dh:(hq0 + 2) * dh] = jnp.concatenate(res, axis=0).T.astype(BF16)
                yield


GDN_LEVELS = GDN_CHUNK.bit_length() - 1


def _gdn_chunk_local_steps(g_ref, ba_ref, gp_ref, staged):
    tt = g_ref.shape[0]
    nchunk = tt // GDN_CHUNK
    hd = GDN_HEAD_DIM
    heads = range(GDN_HEADS)
    dot = functools.partial(jnp.dot, preferred_element_type=F32)

    ba = ba_ref[...]
    beta_all = 1.0 / (1.0 + jnp.exp(-ba))
    pre = ba + gp_ref[1:2, :]
    softplus = jnp.maximum(pre, 0.0) + jnp.log(1.0 + jnp.exp(-jnp.abs(pre)))
    g_all = -jnp.exp(gp_ref[0:1, :]) * softplus

    ri = lax.broadcasted_iota(jnp.int32, (tt, tt), 0)
    ci = lax.broadcasted_iota(jnp.int32, (tt, tt), 1)
    causal = ((ri // GDN_CHUNK) == (ci // GDN_CHUNK)) & (ri >= ci)

    ltri = jnp.where(causal, 1.0, 0.0).astype(BF16)
    g1, g2, g3 = _split3(g_all)
    gc_all = (dot(ltri, g3) + dot(ltri, g2)) + dot(ltri, g1)
    gc_rows = gc_all.T
    gcl_all = jnp.concatenate(
        [jnp.broadcast_to(gc_all[(c + 1) * GDN_CHUNK - 1:(c + 1) * GDN_CHUNK, :], (GDN_CHUNK, GATE_DIM))
         for c in range(nchunk)], axis=0)
    egc_all = jnp.exp(gc_all)
    ekd_all = jnp.exp(gcl_all - gc_all)

    rc = lax.broadcasted_iota(jnp.int32, (GDN_CHUNK, GDN_CHUNK), 0)
    cc = lax.broadcasted_iota(jnp.int32, (GDN_CHUNK, GDN_CHUNK), 1)
    level_masks = []
    for lv in range(GDN_LEVELS):
        s = 1 << lv
        lower_left = ((rc // (2 * s)) == (cc // (2 * s))) & ((rc // s) > (cc // s))
        level_masks.append(jnp.where(lower_left, 1.0, 0.0).astype(BF16))
    eye = jnp.where(rc == cc, 1.0, 0.0)
    band = jnp.where(rc >= cc, 0.0, NEG_BIG)
    yield

    chunks = [slice(c * GDN_CHUNK, (c + 1) * GDN_CHUNK) for c in range(nchunk)]
    a_lv, qk_d, qd, kd, rhs = [], [], [], [], []
    for h in heads:
        q16 = g_ref[:, h * hd:(h + 1) * hd]
        k16 = g_ref[:, GDN_DIM + h * hd:GDN_DIM + (h + 1) * hd]
        q, k = q16.astype(F32), k16.astype(F32)
        v = g_ref[:, 2 * GDN_DIM + h * hd:2 * GDN_DIM + (h + 1) * hd].astype(F32)
        beta = beta_all[:, h:h + 1]
        gc = gc_all[:, GDN_HEADS + h:GDN_HEADS + h + 1]
        gc_row = gc_rows[GDN_HEADS + h:GDN_HEADS + h + 1, :]
        egc = egc_all[:, GDN_HEADS + h:GDN_HEADS + h + 1]
        kb = k * beta
        kb16 = kb.astype(BF16)
        a_h, qk_h = [], []
        for rs in chunks:
            decay = jnp.exp((gc[rs] - gc_row[:, rs]) + band)
            a_mat = (_mm_nt(kb16[rs], k16[rs]) * decay).astype(BF16)
            qk_h.append((_mm_nt(q16[rs], k16[rs]) * decay).astype(BF16))
            a_h.append([a_mat * m for m in level_masks])
        a_lv.append(a_h)
        qk_d.append(qk_h)
        qd.append((q * egc).astype(BF16))
        kd.append((k * ekd_all[:, GDN_HEADS + h:GDN_HEADS + h + 1]).astype(BF16))
        rhs.append(jnp.concatenate([v * beta, kb * egc], axis=1).astype(BF16))
        yield

    tinv = [[eye - a_lv[h][c][0].astype(F32) for c in range(nchunk)] for h in heads]
    for lv in range(1, GDN_LEVELS):
        for h in heads:
            for c in range(nchunk):
                t16 = tinv[h][c].astype(BF16)
                x = dot(a_lv[h][c][lv], t16).astype(BF16)
                tinv[h][c] = tinv[h][c] - dot(t16, x)
        yield
    for h in heads:
        sols = [dot(tinv[h][c].astype(BF16), rhs[h][chunks[c]]) for c in range(nchunk)]
        wq = [jnp.concatenate([sols[c][:, hd:2 * hd].astype(BF16), qd[h][chunks[c]]], axis=0) for c in range(nchunk)]
        staged.append((jnp.concatenate([sol[:, 0:hd] for sol in sols], axis=0), wq, qk_d[h], kd[h]))
    staged.append(egc_all)
    yield


GDN_LOCAL_STEPS = 1 + GDN_HEADS + (GDN_LEVELS - 1) + 1


def _gdn_recurrence_steps(su_ref, swq_ref, sqk_ref, skd_ref, segc_ref, z_ref, nw_ref, o_ref, state_ref):
    tt = su_ref.shape[1]
    nchunk = tt // GDN_CHUNK
    hd = GDN_HEAD_DIM
    heads = range(GDN_HEADS)
    dot = functools.partial(jnp.dot, preferred_element_type=F32)
    states = [state_ref[h] for h in heads]
    for c in range(nchunk):
        r0, r1 = c * GDN_CHUNK, (c + 1) * GDN_CHUNK
        for h in heads:
            ws = dot(swq_ref[h, c], states[h].astype(BF16))
            v_new = (su_ref[h, r0:r1, :] - ws[0:GDN_CHUNK]).astype(BF16)
            o = ws[GDN_CHUNK:2 * GDN_CHUNK] + dot(sqk_ref[h, c], v_new)
            gl = segc_ref[r1 - 1:r1, GDN_HEADS + h:GDN_HEADS + h + 1]
            states[h] = states[h] * gl + lax.dot_general(skd_ref[h, r0:r1, :], v_new, (((0,), (0,)), ((), ())),
                                                         preferred_element_type=F32)
            o = o * lax.rsqrt(jnp.mean(o * o, axis=-1, keepdims=True) + NORM_EPS) * nw_ref[...]
            zz = z_ref[r0:r1, h * hd:(h + 1) * hd].astype(F32)
            o_ref[r0:r1, h * hd:(h + 1) * hd] = (o * _silu(zz)).astype(BF16)
        if c == nchunk - 1:
            for h in heads:
                state_ref[h] = states[h]
        yield


def _mixer_kernel(sink_ref, q_ref, kvc_ref, kvp_ref, bias_ref, g_ref, ba_ref, gp_ref, z_ref, nw_ref,
                  attn_ref, gdn_ref, kv_ref, state_ref, su_ref, swq_ref, sqk_ref, skd_ref, segc_ref, *, tiles_per_seq, layer):
    i = pl.program_id(0)
    ntiles = pl.num_programs(0) - 1
    cur = jnp.minimum(i, ntiles - 1)
    prev = jnp.maximum(i - 1, 0)

    @pl.when(i == 0)
    def _():
        su_ref[...] = jnp.zeros_like(su_ref)
        swq_ref[...] = jnp.zeros_like(swq_ref)
        sqk_ref[...] = jnp.zeros_like(sqk_ref)
        skd_ref[...] = jnp.zeros_like(skd_ref)
        segc_ref[...] = jnp.zeros_like(segc_ref)

    @pl.when(prev % tiles_per_seq == 0)
    def _():
        state_ref[...] = jnp.zeros_like(state_ref)

    staged = []
    rec = _gdn_recurrence_steps(su_ref, swq_ref, sqk_ref, skd_ref, segc_ref, z_ref, nw_ref, gdn_ref, state_ref)
    loc = _gdn_chunk_local_steps(g_ref, ba_ref, gp_ref, staged)
    att = _attn_steps(sink_ref, layer, q_ref, kvc_ref, kvp_ref, bias_ref, attn_ref, kv_ref, cur % tiles_per_seq == 0)

    def advance(gen, n):
        for _ in range(n):
            next(gen, None)

    nrec = su_ref.shape[1] // GDN_CHUNK
    nloc = GDN_LOCAL_STEPS
    natt = 1 + (q_ref.shape[0] // WINDOW) * ATTN_KV_HEADS * (ATTN_GROUP // 2)
    for c in range(nrec):
        advance(rec, 1)
        advance(loc, -(-nloc // nrec))
        advance(att, -(-natt // nrec))
    advance(loc, nloc)
    advance(att, natt)

    for h in range(GDN_HEADS):
        u, wq, qk_d, kd = staged[h]
        su_ref[h] = u
        skd_ref[h] = kd
        for c in range(nrec):
            swq_ref[h, c] = wq[c]
            sqk_ref[h, c] = qk_d[c]
    segc_ref[...] = staged[GDN_HEADS]


def _gate_params(a_log, dt_bias):
    gp = jnp.zeros((a_log.shape[0], SUBLANES, GATE_DIM), F32)
    gp = gp.at[:, 0, GDN_HEADS:2 * GDN_HEADS].set(a_log.astype(F32))
    return gp.at[:, 1, GDN_HEADS:2 * GDN_HEADS].set(dt_bias.astype(F32))


def _mixers(qkv, sinks, bias, g, z, ba, gate_params, norm_w, layer, batch, seq):
    m = g.shape[0]
    tt = min(MIX_TT, seq)
    nt = seq // tt
    ntiles = batch * nt
    sub = tt // WINDOW
    nchunk = tt // GDN_CHUNK
    kv_col = ATTN_Q_DIM // (2 * ATTN_KV_DIM)
    cur = lambda i: (jnp.minimum(i, ntiles - 1), 0)
    prev = lambda i: (jnp.maximum(i - 1, 0), 0)
    hd = GDN_HEAD_DIM
    return pl.pallas_call(
        functools.partial(_mixer_kernel, tiles_per_seq=nt, layer=layer),
        grid=(ntiles + 1,),
        in_specs=[pl.BlockSpec(memory_space=pltpu.SMEM),
                  pl.BlockSpec((tt, ATTN_Q_DIM), cur),
                  pl.BlockSpec((tt, 2 * ATTN_KV_DIM), lambda i: (jnp.minimum(i, ntiles - 1), kv_col)),
                  pl.BlockSpec((WINDOW, 2 * ATTN_KV_DIM),
                               lambda i: (jnp.maximum(jnp.minimum(i, ntiles - 1) * sub - 1, 0), kv_col)),
                  pl.BlockSpec((2, ATTN_HEADS, 2 * WINDOW, WINDOW), lambda i: (0, 0, 0, 0)),
                  pl.BlockSpec((tt, GDN_QKV_DIM), cur),
                  pl.BlockSpec((tt, GATE_DIM), cur),
                  pl.BlockSpec((None, SUBLANES, GATE_DIM), lambda i: (layer, 0, 0)),
                  pl.BlockSpec((tt, GDN_DIM), prev),
                  pl.BlockSpec((None, 1, GDN_HEAD_DIM), lambda i: (layer, 0, 0))],
        out_specs=[pl.BlockSpec((tt, ATTN_Q_DIM), cur),
                   pl.BlockSpec((tt, GDN_DIM), prev)],
        out_shape=[jax.ShapeDtypeStruct((m, ATTN_Q_DIM), BF16),
                   jax.ShapeDtypeStruct((m, GDN_DIM), BF16)],
        scratch_shapes=[pltpu.VMEM((tt + WINDOW, 2 * ATTN_KV_DIM), BF16),
                        pltpu.VMEM((GDN_HEADS, hd, hd), F32),
                        pltpu.VMEM((GDN_HEADS, tt, hd), F32),
                        pltpu.VMEM((GDN_HEADS, nchunk, 2 * GDN_CHUNK, hd), BF16),
                        pltpu.VMEM((GDN_HEADS, nchunk, GDN_CHUNK, GDN_CHUNK), BF16),
                        pltpu.VMEM((GDN_HEADS, tt, hd), BF16),
                        pltpu.VMEM((tt, GATE_DIM), F32)],
        compiler_params=pltpu.CompilerParams(dimension_semantics=("arbitrary",),
                                             vmem_limit_bytes=VMEM_LIMIT),
        name="mixers",
    )(sinks, qkv, qkv, qkv, bias, g, ba, gate_params, z, norm_w)


def _mix_ffn_kernel(x_ref, attn_ref, gdn_ref, wout_ref, nw_ref, win_ref, cw_ref, cb_ref, wd_ref, fnw_ref,
                    o_ref, gbuf_ref, carry_ref, act_ref, *, final):
    tm = x_ref.shape[0]
    halo = SUBLANES

    @pl.when(pl.program_id(1) == 0)
    def _():
        carry_ref[...] = jnp.zeros_like(carry_ref)

    def rms(x, w):
        return x * lax.rsqrt(jnp.mean(x * x, axis=-1, keepdims=True) + NORM_EPS) * w

    dot = functools.partial(jnp.dot, preferred_element_type=F32)
    x1 = (x_ref[...] + dot(attn_ref[...], wout_ref[0:ATTN_Q_DIM, :])
          + dot(gdn_ref[...], wout_ref[ATTN_Q_DIM:MIX_DIM, :]))
    h = rms(x1, nw_ref[0:1, :]).astype(BF16)
    for c in range(D_FF // FFN_FC):
        c0, c1 = c * FFN_FC, (c + 1) * FFN_FC
        gate = dot(h, win_ref[:, c0:c1])
        up = dot(h, win_ref[:, D_FF + c0:D_FF + c1])
        gbuf_ref[0:halo, :] = carry_ref[:, c0:c1]
        gbuf_ref[halo:halo + tm, :] = gate
        carry_ref[:, c0:c1] = gate[tm - halo:tm, :]
        conv = gate * cw_ref[2, 0:1, c0:c1] + cb_ref[0:1, c0:c1]
        conv = conv + gbuf_ref[halo - 1:halo - 1 + tm, :] * cw_ref[1, 0:1, c0:c1]
        conv = conv + gbuf_ref[halo - 2:halo - 2 + tm, :] * cw_ref[0, 0:1, c0:c1]
        act_ref[:, c0:c1] = (_silu(conv) * up).astype(BF16)
    out = x1 + dot(act_ref[...], wd_ref[...])
    if final:
        out = rms(out, fnw_ref[0:1, :])
    o_ref[...] = out


def _mix_ffn(x, attn, gdn, w_out, norm_w8, w_ffn_in, conv_w8, conv_b8, w_down, final_norm8, layer, batch, seq, final):
    m = x.shape[0]
    tm = min(FFN_TM, seq)
    nt = seq // tm
    row = lambda b, t: (b * nt + t, 0)
    at_layer = lambda b, t: (layer, 0, 0)
    resident = lambda shape: pl.BlockSpec((None,) + shape, at_layer, pipeline_mode=pl.Buffered(1))
    return pl.pallas_call(
        functools.partial(_mix_ffn_kernel, final=final),
        grid=(batch, nt),
        in_specs=[pl.BlockSpec((tm, D_MODEL), row),
                  pl.BlockSpec((tm, ATTN_Q_DIM), row),
                  pl.BlockSpec((tm, GDN_DIM), row),
                  resident((MIX_DIM, D_MODEL)),
                  pl.BlockSpec((None, SUBLANES, D_MODEL), at_layer),
                  resident((D_MODEL, 2 * D_FF)),
                  pl.BlockSpec((None, FFN_CONV, SUBLANES, D_FF), lambda b, t: (layer, 0, 0, 0)),
                  pl.BlockSpec((None, SUBLANES, D_FF), at_layer),
                  resident((D_FF, D_MODEL)),
                  pl.BlockSpec((SUBLANES, D_MODEL), lambda b, t: (0, 0))],
        out_specs=pl.BlockSpec((tm, D_MODEL), row),
        out_shape=jax.ShapeDtypeStruct((m, D_MODEL), F32),
        scratch_shapes=[pltpu.VMEM((tm + SUBLANES, FFN_FC), F32),
                        pltpu.VMEM((SUBLANES, D_FF), F32),
                        pltpu.VMEM((tm, D_FF), BF16)],
        compiler_params=pltpu.CompilerParams(dimension_semantics=("parallel", "arbitrary"),
                                             vmem_limit_bytes=VMEM_LIMIT),
        name="mix_ffn",
    )(x, attn, gdn, w_out, norm_w8, w_ffn_in, conv_w8, conv_b8, w_down, final_norm8)


def kernel(x, attn_norm, w_in, attn_sinks, gdn_conv_w, gdn_a_log, gdn_dt_bias, gdn_norm, w_out, ffn_norm,
           w_ffn_in, ffn_conv_w, ffn_conv_b, w_down, final_norm):
    batch, seq, _ = x.shape
    assert x.shape[-1] == D_MODEL and seq % WINDOW == 0 and seq % GDN_CHUNK == 0
    for tile in (IN_TM, MIX_TT, FFN_TM):
        assert seq % min(tile, seq) == 0 and min(tile, seq) % WINDOW == 0 and min(tile, seq) % GDN_CHUNK == 0
    xf = x.reshape(batch * seq, D_MODEL).astype(F32)
    bias = _attn_bias()
    depth = w_in.shape[0]
    w_in_b = jnp.pad(w_in, ((0, 0), (0, 0), (0, IN_PAD - IN_DIM))).astype(BF16)
    w_out_b, w_ffn_in_b, w_down_b = w_out.astype(BF16), w_ffn_in.astype(BF16), w_down.astype(BF16)
    attn_norm8, gdn_conv8 = _rows8(attn_norm), _rows8(gdn_conv_w)
    ffn_norm8, ffn_conv8, ffn_bias8, final_norm8 = _rows8(ffn_norm), _rows8(ffn_conv_w), _rows8(ffn_conv_b), _rows8(final_norm)
    gate_params = _gate_params(gdn_a_log, gdn_dt_bias)
    sinks = attn_sinks.astype(F32)
    gdn_norm3 = gdn_norm.astype(F32)[:, None, :]
    for l in range(depth):
        qkv, g, z, ba = _inproj(xf, attn_norm8, w_in_b, gdn_conv8, l, batch, seq)
        attn, gdn = _mixers(qkv, sinks, bias, g, z, ba, gate_params, gdn_norm3, l, batch, seq)
        xf = _mix_ffn(xf, attn, gdn, w_out_b, ffn_norm8, w_ffn_in_b, ffn_conv8, ffn_bias8, w_down_b, final_norm8,
                      l, batch, seq, final=(l == depth - 1))
    return xf.reshape(batch, seq, D_MODEL).astype(x.dtype)
```
